```python
import math
import jax
import jax.numpy as jnp
from jax import lax
import numpy as np

D_MODEL = 1024
BATCH = 4
SEQ = 8192
DEPTH = 2

N_MIXERS = 4
N_HEADS = 16
GROUP_HEADS = N_HEADS // N_MIXERS
HEAD_DIM = D_MODEL // N_HEADS
GROUP_WIDTH = GROUP_HEADS * HEAD_DIM

MOBA_BLOCK = 256
MOBA_TOPK = 3

NSA_KV_HEADS = 1
NSA_CMP_LEN = 32
NSA_CMP_STRIDE = 16
NSA_CMP_HIDDEN = 2 * HEAD_DIM
NSA_SEL_BLOCK = 64
NSA_SEL_TOPN = 16
NSA_WINDOW = 512

DIFF_QK_DIM = HEAD_DIM // 2

SWA_KV_HEADS = 2
SWA_WINDOW = 128

D_FF = 4 * D_MODEL
CONV_WIDTH = 3

Q_BLOCK = 128
GATHER_Q_BLOCK = 32
NORM_EPS = 1e-6
NEG_INF = -1e30
TINY = 1e-30
FORCED_SCORE = 1e4

NSA_KV_WIDTH = NSA_KV_HEADS * HEAD_DIM
SWA_KV_WIDTH = SWA_KV_HEADS * HEAD_DIM
IN_SPLITS = (GROUP_WIDTH, GROUP_WIDTH, GROUP_WIDTH,
             GROUP_WIDTH, NSA_KV_WIDTH, NSA_KV_WIDTH, NSA_KV_WIDTH,
             NSA_KV_WIDTH, NSA_KV_WIDTH, NSA_KV_WIDTH, 3 * GROUP_HEADS,
             GROUP_WIDTH, GROUP_WIDTH, GROUP_WIDTH,
             GROUP_WIDTH, SWA_KV_WIDTH, SWA_KV_WIDTH)
D_IN = sum(IN_SPLITS)

kernel_name = 'hybrid_moba_nsa_diff_swa_convffn'


def _rms_norm(x, gain):
    xf = x.astype(jnp.float32)
    y = xf * lax.rsqrt(jnp.mean(xf * xf, axis=-1, keepdims=True) + NORM_EPS)
    return (y * gain.astype(jnp.float32)).astype(x.dtype)


def _alibi_slopes(mixer):
    slopes = np.power(np.float32(2.0), np.arange(1, N_HEADS + 1, dtype=np.float32) * np.float32(-8.0 / N_HEADS))
    return jnp.asarray(slopes[mixer::N_MIXERS], dtype=jnp.float32)


def _masked_softmax(s, mask, sink=None):
    s = jnp.where(mask, s, NEG_INF)
    m = jnp.max(s, axis=-1, keepdims=True)
    if sink is not None:
        m = jnp.maximum(m, sink)
    p = jnp.where(mask, jnp.exp(s - m), 0.0)
    denom = jnp.sum(p, axis=-1, keepdims=True)
    if sink is not None:
        denom = denom + jnp.exp(sink - m)
    return p / jnp.maximum(denom, TINY)


def _heads(t, n):
    b, s, _ = t.shape
    return t.reshape(b, s, n, -1).transpose(0, 2, 1, 3)


def _merge(t):
    b, h, s, d = t.shape
    return t.transpose(0, 2, 1, 3).reshape(b, s, h * d)


def _to_chunks(t, qb, axis):
    shp = t.shape
    t = t.reshape(shp[:axis] + (shp[axis] // qb, qb) + shp[axis + 1:])
    return jnp.moveaxis(t, axis, 0)


def _from_chunks(t, axis):
    t = jnp.moveaxis(t, 0, axis)
    shp = t.shape
    return t.reshape(shp[:axis] + (shp[axis] * shp[axis + 1],) + shp[axis + 2:])


def _gather_blocks(blocks, idx):
    g = jax.vmap(jax.vmap(lambda bl, i: bl[i]))(blocks, idx)
    b, h, q, k, l, d = g.shape
    return g.reshape(b, h, q, k * l, d)


def _banded_attention(q, k, v, slopes, window, sinks=None):
    b, g, r, s, dh = q.shape
    nb = s // Q_BLOCK
    n_prev = -(-window // Q_BLOCK)

    def bands(t):
        tp = jnp.pad(t, ((0, 0), (0, 0), (n_prev * Q_BLOCK, 0), (0, 0))).reshape(b, g, nb + n_prev, Q_BLOCK, dh)
        return jnp.concatenate([tp[:, :, p:p + nb] for p in range(n_prev + 1)], axis=3)

    kb, vb = bands(k), bands(v)
    qpos = jnp.arange(s).reshape(nb, Q_BLOCK)
    kpos = (jnp.arange(nb)[:, None] - n_prev) * Q_BLOCK + jnp.arange((n_prev + 1) * Q_BLOCK)
    dist = qpos[:, :, None] - kpos[:, None, :]
    mask = (kpos[:, None, :] >= 0) & (dist >= 0) & (dist < window)
    qr = q.reshape(b, g, r, nb, Q_BLOCK, dh)
    sc = jnp.einsum('bgrnqd,bgnkd->bgrnqk', qr, kb).astype(jnp.float32) * (dh ** -0.5)
    sc = sc - slopes[:, :, None, None, None] * dist.astype(jnp.float32)
    sink = None if sinks is None else sinks.astype(jnp.float32)[:, :, None, None, None]
    p = _masked_softmax(sc, mask, sink)
    o = jnp.einsum('bgrnqk,bgnkd->bgrnqd', p.astype(vb.dtype), vb)
    return o.reshape(b, g, r, s, dh)


def _moba_attention(q, k, v, slopes):
    b, h, s, dh = q.shape
    nb = -(-s // MOBA_BLOCK)
    pad = nb * MOBA_BLOCK - s
    kp = jnp.pad(k, ((0, 0), (0, 0), (0, pad), (0, 0))).reshape(b, h, nb, MOBA_BLOCK, dh)
    vp = jnp.pad(v, ((0, 0), (0, 0), (0, pad), (0, 0))).reshape(b, h, nb, MOBA_BLOCK, dh)
    pos = jnp.arange(s)
    qblk = pos // MOBA_BLOCK
    own = jnp.broadcast_to(qblk[None, None, :, None], (b, h, s, 1))
    n_top = min(MOBA_TOPK, nb - 1)
    if n_top > 0:
        kmean = jnp.mean(kp.astype(jnp.float32), axis=3)
        gate = jnp.einsum('bhsd,bhnd->bhsn', q.astype(jnp.float32), kmean)
        past = jnp.arange(nb)[None, :] < qblk[:, None]
        gate = jnp.where(past, gate, NEG_INF)
        top_val, top_idx = lax.top_k(gate, n_top)
        sel_idx = jnp.concatenate([top_idx, own], axis=-1)
        sel_ok = jnp.concatenate([top_val > 0.5 * NEG_INF, jnp.ones(own.shape, dtype=bool)], axis=-1)
    else:
        sel_idx = own
        sel_ok = jnp.ones(own.shape, dtype=bool)
    scale = dh ** -0.5
    offs = jnp.arange(MOBA_BLOCK)

    def chunk(args):
        qc, ic, okc, tc = args
        kg = _gather_blocks(kp, ic)
        vg = _gather_blocks(vp, ic)
        kpos = (ic[..., None] * MOBA_BLOCK + offs).reshape(ic.shape[:3] + (-1,))
        ok = jnp.repeat(okc, MOBA_BLOCK, axis=-1)
        sc = jnp.einsum('bhqd,bhqkd->bhqk', qc, kg).astype(jnp.float32) * scale
        sc = sc - slopes[None, :, None, None] * (tc[:, None] - kpos).astype(jnp.float32)
        p = _masked_softmax(sc, ok & (kpos <= tc[:, None]))
        return jnp.einsum('bhqk,bhqkd->bhqd', p.astype(vg.dtype), vg)

    out = lax.map(chunk, (_to_chunks(q, GATHER_Q_BLOCK, 2), _to_chunks(sel_idx, GATHER_Q_BLOCK, 2),
                          _to_chunks(sel_ok, GATHER_Q_BLOCK, 2), pos.reshape(-1, GATHER_Q_BLOCK)))
    return _from_chunks(out, 2)


def _nsa_compress(x, pos_emb, w1, b1, w2):
    b, g, s, dh = x.shape
    n_sub = NSA_CMP_LEN // NSA_CMP_STRIDE
    c = x.reshape(b, g, s // NSA_CMP_STRIDE, NSA_CMP_STRIDE, dh)
    nc = c.shape[2] - n_sub + 1
    win = jnp.concatenate([c[:, :, i:i + nc] for i in range(n_sub)], axis=3)
    hid = jax.nn.gelu(jnp.einsum('bgnld,ldf->bgnf', win + pos_emb, w1) + b1)
    return jnp.einsum('bgnf,fd->bgnd', hid, w2)


def _cmp_to_sel_matrix(nc, nsel):
    cs = np.arange(nc)[:, None] * NSA_CMP_STRIDE
    bs = np.arange(nsel)[None, :] * NSA_SEL_BLOCK
    ov = np.clip(np.minimum(cs + NSA_CMP_LEN, bs + NSA_SEL_BLOCK) - np.maximum(cs, bs), 0, None)
    return (ov / NSA_CMP_LEN).astype(np.float32)


def _nsa_attention(q, k_cmp, v_cmp, k_slc, v_slc, k_win, v_win, gates, slopes, cmp_k, cmp_v):
    b, h, s, dh = q.shape
    g = NSA_KV_HEADS
    r = h // g
    qg = q.reshape(b, g, r, s, dh)
    sl = slopes.reshape(g, r)
    kc = _nsa_compress(k_cmp, *cmp_k)
    vc = _nsa_compress(v_cmp, *cmp_v)
    nc = kc.shape[2]
    cmp_end = jnp.arange(nc) * NSA_CMP_STRIDE + NSA_CMP_LEN - 1
    nsel = s // NSA_SEL_BLOCK
    n_top = min(NSA_SEL_TOPN, nsel)
    overlap = jnp.asarray(_cmp_to_sel_matrix(nc, nsel))
    ks = k_slc.reshape(b, g, nsel, NSA_SEL_BLOCK, dh)
    vs = v_slc.reshape(b, g, nsel, NSA_SEL_BLOCK, dh)
    blk_ids = jnp.arange(nsel)
    offs = jnp.arange(NSA_SEL_BLOCK)
    scale = dh ** -0.5

    def chunk(args):
        qc, tc = args
        sc = jnp.einsum('bgrqd,bgnd->bgrqn', qc, kc).astype(jnp.float32) * scale
        sc = sc - sl[:, :, None, None] * (tc[:, None] - cmp_end).astype(jnp.float32)
        p_c = _masked_softmax(sc, cmp_end <= tc[:, None])
        o_c = jnp.einsum('bgrqn,bgnd->bgrqd', p_c.astype(vc.dtype), vc)
        imp = jnp.einsum('bgrqn,ns->bgqs', p_c, overlap)
        qblk = tc // NSA_SEL_BLOCK
        forced = (blk_ids == 0) | (blk_ids == qblk[:, None]) | (blk_ids == qblk[:, None] - 1)
        score = jnp.where(forced, FORCED_SCORE, imp)
        score = jnp.where(blk_ids * NSA_SEL_BLOCK <= tc[:, None], score, NEG_INF)
        top_val, top_idx = lax.top_k(score, n_top)
        ok = jnp.repeat(top_val > 0.5 * NEG_INF, NSA_SEL_BLOCK, axis=-1)
        kg = _gather_blocks(ks, top_idx)
        vg = _gather_blocks(vs, top_idx)
        kpos = (top_idx[..., None] * NSA_SEL_BLOCK + offs).reshape(top_idx.shape[:3] + (-1,))
        dist = (tc[:, None] - kpos)[:, :, None]
        s2 = jnp.einsum('bgrqd,bgqkd->bgrqk', qc, kg).astype(jnp.float32) * scale
        s2 = s2 - sl[:, :, None, None] * dist.astype(jnp.float32)
        p_s = _masked_softmax(s2, ok[:, :, None] & (dist >= 0))
        o_s = jnp.einsum('bgrqk,bgqkd->bgrqd', p_s.astype(vg.dtype), vg)
        return o_c, o_s

    o_c, o_s = lax.map(chunk, (_to_chunks(qg, GATHER_Q_BLOCK, 3), jnp.arange(s).reshape(-1, GATHER_Q_BLOCK)))
    o_c = _from_chunks(o_c, 3).reshape(b, h, s, dh)
    o_s = _from_chunks(o_s, 3).reshape(b, h, s, dh)
    o_w = _banded_attention(qg, k_win, v_win, sl, NSA_WINDOW).reshape(b, h, s, dh)
    return gates[..., 0:1] * o_c + gates[..., 1:2] * o_s + gates[..., 2:3] * o_w


def _diff_attention(q1, q2, k1, k2, v, slopes, lq1, lk1, lq2, lk2, subln, lambda_init):
    b, h, s, dq = q1.shape
    scale = dq ** -0.5
    lam = (jnp.exp(jnp.sum(lq1.astype(jnp.float32) * lk1.astype(jnp.float32)))
           - jnp.exp(jnp.sum(lq2.astype(jnp.float32) * lk2.astype(jnp.float32))) + lambda_init)
    kpos = jnp.arange(s)

    def chunk(args):
        a1, a2, tc = args
        dist = (tc[:, None] - kpos).astype(jnp.float32)
        mask = dist >= 0
        bias = -slopes[:, None, None] * dist
        s1 = jnp.einsum('bhqd,bhkd->bhqk', a1, k1).astype(jnp.float32) * scale + bias
        s2 = jnp.einsum('bhqd,bhkd->bhqk', a2, k2).astype(jnp.float32) * scale + bias
        p = _masked_softmax(s1, mask) - lam * _masked_softmax(s2, mask)
        return jnp.einsum('bhqk,bhkd->bhqd', p.astype(v.dtype), v)

    o = lax.map(chunk, (_to_chunks(q1, Q_BLOCK, 2), _to_chunks(q2, Q_BLOCK, 2), kpos.reshape(-1, Q_BLOCK)))
    o = _from_chunks(o, 2)
    return _rms_norm(o, subln) * (1.0 - lambda_init)


def _swa_attention(q, k, v, slopes, sinks):
    b, h, s, dh = q.shape
    g = SWA_KV_HEADS
    r = h // g
    o = _banded_attention(q.reshape(b, g, r, s, dh), k, v, slopes.reshape(g, r), SWA_WINDOW, sinks.reshape(g, r))
    return o.reshape(b, h, s, dh)


def _conv_ffn(x, w_gate, w_up, conv_w, conv_b, w_down):
    a = x @ w_gate
    a = lax.conv_general_dilated(a, conv_w[:, None, :], window_strides=(1,), padding=[(CONV_WIDTH - 1, 0)],
                                 dimension_numbers=('NWC', 'WIO', 'NWC'), feature_group_count=a.shape[-1]) + conv_b
    return (jax.nn.gelu(a, approximate=True) * (x @ w_up)) @ w_down


def setup_inputs(seed: int = 0) -> dict:
    key = jax.random.key(seed)
    keys = iter(jax.random.split(key, 40))

    def nrm(shape, scale):
        return jax.random.normal(next(keys), shape, jnp.float32) * scale

    L = DEPTH
    return {
        'x': nrm((BATCH, SEQ, D_MODEL), 1.0),
        'attn_pre_norm': 1.0 + nrm((L, D_MODEL), 0.02),
        'attn_post_norm': 1.0 + nrm((L, D_MODEL), 0.02),
        'ffn_pre_norm': 1.0 + nrm((L, D_MODEL), 0.02),
        'ffn_post_norm': 1.0 + nrm((L, D_MODEL), 0.02),
        'w_in': nrm((L, D_MODEL, D_IN), D_MODEL ** -0.5),
        'w_out': nrm((L, D_MODEL, D_MODEL), D_MODEL ** -0.5),
        'nsa_cmp_pos_k': nrm((L, NSA_CMP_LEN, HEAD_DIM), 0.1),
        'nsa_cmp_w1_k': nrm((L, NSA_CMP_LEN, HEAD_DIM, NSA_CMP_HIDDEN), (NSA_CMP_LEN * HEAD_DIM) ** -0.5),
        'nsa_cmp_b1_k': nrm((L, NSA_CMP_HIDDEN), 0.01),
        'nsa_cmp_w2_k': nrm((L, NSA_CMP_HIDDEN, HEAD_DIM), NSA_CMP_HIDDEN ** -0.5),
        'nsa_cmp_pos_v': nrm((L, NSA_CMP_LEN, HEAD_DIM), 0.1),
        'nsa_cmp_w1_v': nrm((L, NSA_CMP_LEN, HEAD_DIM, NSA_CMP_HIDDEN), (NSA_CMP_LEN * HEAD_DIM) ** -0.5),
        'nsa_cmp_b1_v': nrm((L, NSA_CMP_HIDDEN), 0.01),
        'nsa_cmp_w2_v': nrm((L, NSA_CMP_HIDDEN, HEAD_DIM), NSA_CMP_HIDDEN ** -0.5),
        'diff_lambda_q1': nrm((L, DIFF_QK_DIM), 0.1),
        'diff_lambda_k1': nrm((L, DIFF_QK_DIM), 0.1),
        'diff_lambda_q2': nrm((L, DIFF_QK_DIM), 0.1),
        'diff_lambda_k2': nrm((L, DIFF_QK_DIM), 0.1),
        'diff_subln': 1.0 + nrm((L, HEAD_DIM), 0.02),
        'swa_sinks': nrm((L, GROUP_HEADS), 0.5),
        'ffn_w_gate': nrm((L, D_MODEL, D_FF), D_MODEL ** -0.5),
        'ffn_w_up': nrm((L, D_MODEL, D_FF), D_MODEL ** -0.5),
        'ffn_conv_w': nrm((L, CONV_WIDTH, D_FF), CONV_WIDTH ** -0.5),
        'ffn_conv_b': nrm((L, D_FF), 0.01),
        'ffn_w_down': nrm((L, D_FF, D_MODEL), D_FF ** -0.5),
    }


def reference(x, attn_pre_norm, attn_post_norm, ffn_pre_norm, ffn_post_norm, w_in, w_out,
              nsa_cmp_pos_k, nsa_cmp_w1_k, nsa_cmp_b1_k, nsa_cmp_w2_k,
              nsa_cmp_pos_v, nsa_cmp_w1_v, nsa_cmp_b1_v, nsa_cmp_w2_v,
              diff_lambda_q1, diff_lambda_k1, diff_lambda_q2, diff_lambda_k2, diff_subln,
              swa_sinks, ffn_w_gate, ffn_w_up, ffn_conv_w, ffn_conv_b, ffn_w_down):
    b, s, _ = x.shape
    splits = np.cumsum(IN_SPLITS)[:-1].tolist()
    for l in range(DEPTH):
        h = _rms_norm(x, attn_pre_norm[l])
        (mq, mk, mv, nq, nkc, nvc, nks, nvs, nkw, nvw, ngate,
         dq, dk, dv, sq, sk, sv) = jnp.split(h @ w_in[l], splits, axis=-1)
        o_moba = _moba_attention(_heads(mq, GROUP_HEADS), _heads(mk, GROUP_HEADS), _heads(mv, GROUP_HEADS),
                                 _alibi_slopes(0))
        gates = jax.nn.sigmoid(ngate.reshape(b, s, GROUP_HEADS, 3).transpose(0, 2, 1, 3))
        o_nsa = _nsa_attention(_heads(nq, GROUP_HEADS), _heads(nkc, NSA_KV_HEADS), _heads(nvc, NSA_KV_HEADS),
                               _heads(nks, NSA_KV_HEADS), _heads(nvs, NSA_KV_HEADS),
                               _heads(nkw, NSA_KV_HEADS), _heads(nvw, NSA_KV_HEADS), gates, _alibi_slopes(1),
                               (nsa_cmp_pos_k[l], nsa_cmp_w1_k[l], nsa_cmp_b1_k[l], nsa_cmp_w2_k[l]),
                               (nsa_cmp_pos_v[l], nsa_cmp_w1_v[l], nsa_cmp_b1_v[l], nsa_cmp_w2_v[l]))
        dq4 = _heads(dq, GROUP_HEADS)
        dk4 = _heads(dk, GROUP_HEADS)
        lambda_init = 0.8 - 0.6 * math.exp(-0.3 * l)
        o_diff = _diff_attention(dq4[..., :DIFF_QK_DIM], dq4[..., DIFF_QK_DIM:],
                                 dk4[..., :DIFF_QK_DIM], dk4[..., DIFF_QK_DIM:], _heads(dv, GROUP_HEADS),
                                 _alibi_slopes(2), diff_lambda_q1[l], diff_lambda_k1[l],
                                 diff_lambda_q2[l], diff_lambda_k2[l], diff_subln[l], lambda_init)
        o_swa = _swa_attention(_heads(sq, GROUP_HEADS), _heads(sk, SWA_KV_HEADS), _heads(sv, SWA_KV_HEADS),
                               _alibi_slopes(3), swa_sinks[l])
        mix = jnp.concatenate([_merge(o_moba), _merge(o_nsa), _merge(o_diff), _merge(o_swa)], axis=-1)
        x = x + _rms_norm(mix @ w_out[l], attn_post_norm[l])
        h = _rms_norm(x, ffn_pre_norm[l])
        x = x + _rms_norm(_conv_ffn(h, ffn_w_gate[l], ffn_w_up[l], ffn_conv_w[l], ffn_conv_b[l], ffn_w_down[l]),
                          ffn_post_norm[l])
    return x
```

```python
import functools
import math

import numpy as np
import jax
import jax.numpy as jnp
from jax import lax
from jax.experimental import pallas as pl
from jax.experimental.pallas import tpu as pltpu

F32 = jnp.float32
BF16 = jnp.bfloat16
HIGHEST = lax.Precision.HIGHEST

N_MIXERS = 4
N_HEADS = 16
GROUP_HEADS = 4
HEAD_DIM = 64
GROUP_WIDTH = 256
MOBA_BLOCK = 256
MOBA_TOPK = 3
NSA_CMP_LEN = 32
NSA_CMP_STRIDE = 16
NSA_SEL_BLOCK = 64
NSA_SEL_TOPN = 16
NSA_WINDOW = 512
DIFF_QK_DIM = 32
SWA_WINDOW = 128
CONV_WIDTH = 3
NORM_EPS = 1e-6
NEG_INF = -1e30
TINY = 1e-30
FORCED_SCORE = 1e4
REMOVED = -3e38

TQ = 256
SEL_ROWS = 128
VMEM_LIMIT = 56 * 1024 * 1024


def _alibi_slopes(mixer):
    slopes = np.power(np.float32(2.0), np.arange(1, N_HEADS + 1, dtype=np.float32) * np.float32(-8.0 / N_HEADS))
    return [float(s) for s in slopes[mixer::N_MIXERS]]


def _nt(a, b, precision=None):
    return lax.dot_general(a, b, (((1,), (1,)), ((), ())), preferred_element_type=F32, precision=precision)


def _mm(a, b, precision=None):
    return jnp.dot(a, b, preferred_element_type=F32, precision=precision)


def _rms(x, gain):
    return (x * lax.rsqrt(jnp.mean(x * x, axis=-1, keepdims=True) + NORM_EPS)) * gain


def _params(*sem):
    return pltpu.CompilerParams(dimension_semantics=sem, vmem_limit_bytes=VMEM_LIMIT)


_IN_GROUPS = (
    ("mq", 256, BF16), ("mk", 256, BF16), ("mv", 256, BF16),
    ("nq", 256, BF16), ("ncmp", 128, F32),
    ("nks", 256, BF16), ("nvs", 256, BF16), ("nkw", 256, BF16), ("nvw", 256, BF16),
    ("gates", 768, F32),
    ("dq", 256, BF16), ("dk", 256, BF16), ("dv", 256, BF16),
    ("sq", 256, BF16), ("sk", 256, BF16), ("sv", 256, BF16),
)
_IN_WIDTH = sum(w for _, w, _ in _IN_GROUPS)


def _in_columns():
    r = np.arange
    cols = [r(0, 1024),
            r(1024, 1152)]
    for off in (1152, 1216, 1280, 1344):
        cols.append(np.tile(r(off, off + 64), 4))
    gate = [1408 + 3 * h + j for j in range(3) for h in range(4) for _ in range(64)]
    cols.append(np.asarray(gate))
    cols.append(r(1420, 2188))
    cols.append(r(2188, 2444))
    for off in (2444, 2572):
        cols.append(np.concatenate([r(off + 64 * (h // 2), off + 64 * (h // 2) + 64) for h in range(4)]))
    cols = np.concatenate(cols)
    assert cols.shape[0] == _IN_WIDTH
    return cols


def _in_proj_kernel(x_ref, g_ref, w_ref, *out_refs, tm):
    outs = dict(zip([n for n, _, _ in _IN_GROUPS] + ["kmean"], out_refs))
    hb = _rms(x_ref[...], g_ref[...]).astype(BF16)
    off = 0
    for name, width, dtype in _IN_GROUPS:
        res = _mm(hb, w_ref[:, off:off + width])
        off += width
        if name == "gates":
            res = 1.0 / (1.0 + jnp.exp(-res))
        outs[name][...] = res.astype(dtype)
        if name == "mk":
            for r in range(tm // MOBA_BLOCK):
                blk = res[r * MOBA_BLOCK:(r + 1) * MOBA_BLOCK]
                outs["kmean"][0, r:r + 1, :] = jnp.mean(blk, axis=0, keepdims=True)


def _in_proj(x2, gain, w):
    n, d = x2.shape
    tm = 512
    out_shape = [jax.ShapeDtypeStruct((n, wd), dt) for _, wd, dt in _IN_GROUPS]
    out_specs = [pl.BlockSpec((tm, wd), lambda i: (i, 0)) for _, wd, _ in _IN_GROUPS]
    out_shape.append(jax.ShapeDtypeStruct((n // tm, tm // MOBA_BLOCK, 256), F32))
    out_specs.append(pl.BlockSpec((1, tm // MOBA_BLOCK, 256), lambda i: (i, 0, 0)))
    outs = pl.pallas_call(
        functools.partial(_in_proj_kernel, tm=tm),
        grid=(n // tm,),
        in_specs=[pl.BlockSpec((tm, d), lambda i: (i, 0)),
                  pl.BlockSpec((1, d), lambda i: (0, 0)),
                  pl.BlockSpec((d, _IN_WIDTH), lambda i: (0, 0))],
        out_specs=out_specs,
        out_shape=out_shape,
        compiler_params=_params("parallel"),
        name="in_proj",
    )(x2, gain.reshape(1, d), w)
    res = dict(zip([nm for nm, _, _ in _IN_GROUPS] + ["kmean"], outs))
    return res


def _head_mask(h, lo=0, hi=HEAD_DIM):
    lane = lax.broadcasted_iota(jnp.int32, (1, GROUP_WIDTH), 1)
    return (lane >= h * HEAD_DIM + lo) & (lane < h * HEAD_DIM + hi)


def _rel_pos(tq):
    r = lax.broadcasted_iota(jnp.int32, (tq, tq), 0)
    c = lax.broadcasted_iota(jnp.int32, (tq, tq), 1)
    return r, c


def _flash_init(m_ref, l_ref, acc_ref):
    m_ref[...] = jnp.full(m_ref.shape, NEG_INF, F32)
    l_ref[...] = jnp.zeros(l_ref.shape, F32)
    acc_ref[...] = jnp.zeros(acc_ref.shape, F32)


def _flash_update(u, mask, cj, v, m_ref, l_ref, acc_ref):
    if mask is not None:
        u = jnp.where(mask, u, NEG_INF)
    m_old = m_ref[...]
    m_new = jnp.maximum(m_old, jnp.max(u, axis=-1, keepdims=True) + cj)
    p = jnp.exp(u - (m_new - cj))
    if mask is not None:
        p = jnp.where(mask, p, 0.0)
    alpha = jnp.exp(m_old - m_new)
    l_ref[...] = alpha * l_ref[...] + jnp.sum(p, axis=-1, keepdims=True)
    acc_ref[...] = alpha * acc_ref[...] + _mm(p.astype(BF16), v)
    m_ref[...] = m_new


def _topk_bias_t(score_t, k):
    nrow = score_t.shape[0]
    rows = lax.broadcasted_iota(jnp.int32, score_t.shape, 0).astype(F32)
    bias = jnp.full(score_t.shape, NEG_INF, F32)
    work = score_t
    for _ in range(k):
        m = jnp.max(work, axis=0, keepdims=True)
        idx = jnp.min(jnp.where(work == m, rows, float(nrow)), axis=0, keepdims=True)
        pick = rows == idx
        bias = jnp.where(pick, jnp.where(m > 0.5 * NEG_INF, 0.0, NEG_INF), bias)
        work = jnp.where(pick, REMOVED, work)
    return bias


def _expand_bias(selbias, first_block, blocks_per_tile, tk):
    b = lax.broadcasted_iota(jnp.int32, (SEL_ROWS, tk), 0)
    c = lax.broadcasted_iota(jnp.int32, (SEL_ROWS, tk), 1)
    e = (b == first_block + c // (tk // blocks_per_tile)).astype(F32).astype(BF16)
    return _mm(selbias, e)


def _kv_tile(ref, j, tk):
    return ref[0, pl.ds(pl.multiple_of(j * tk, tk), tk), :]


def _store_head(o_ref, h, val):
    o_ref[:, h * HEAD_DIM:(h + 1) * HEAD_DIM] = val[:, h * HEAD_DIM:(h + 1) * HEAD_DIM].astype(o_ref.dtype)


def _attn_specs(nq, tq, s):
    qspec = pl.BlockSpec((tq, GROUP_WIDTH), lambda b, i: (b * nq + i, 0))
    kvspec = pl.BlockSpec((1, s, GROUP_WIDTH), lambda b, i: (b, 0, 0))
    return qspec, kvspec


def _flash_scratch(tq, n=1):
    out = []
    for _ in range(n):
        out += [pltpu.VMEM((tq, 1), F32), pltpu.VMEM((tq, 1), F32), pltpu.VMEM((tq, GROUP_WIDTH), F32)]
    return out


def _moba_kernel(q_ref, k_ref, v_ref, km_ref, o_ref, m_ref, l_ref, acc_ref, *, slopes, tq):
    i = pl.program_id(1)
    scale = HEAD_DIM ** -0.5
    qf = q_ref[...].astype(F32)
    r, c = _rel_pos(tq)
    rc = (r - c).astype(F32)
    blk_rows = lax.broadcasted_iota(jnp.int32, (SEL_ROWS, tq), 0)
    for h in range(GROUP_HEADS):
        qh = jnp.where(_head_mask(h), qf, 0.0)
        gate_t = _nt(km_ref[0], qh, precision=HIGHEST)
        gate_t = jnp.where(blk_rows < i, gate_t, NEG_INF)
        selbias = _topk_bias_t(gate_t, MOBA_TOPK).T.astype(BF16)
        qs = (qh * scale).astype(BF16)
        table = rc * (-slopes[h])
        _flash_init(m_ref, l_ref, acc_ref)
        u = _nt(qs, _kv_tile(k_ref, i, tq)) + table
        _flash_update(u, c <= r, 0.0, _kv_tile(v_ref, i, tq), m_ref, l_ref, acc_ref)

        def body(j, carry, qs=qs, table=table, selbias=selbias, h=h):
            u = _nt(qs, _kv_tile(k_ref, j, tq)) + table + _expand_bias(selbias, j, 1, tq)
            cj = (i - j).astype(F32) * (-slopes[h] * tq)
            _flash_update(u, None, cj, _kv_tile(v_ref, j, tq), m_ref, l_ref, acc_ref)
            return carry

        lax.fori_loop(0, i, body, 0)
        _store_head(o_ref, h, acc_ref[...] / l_ref[...])


def _moba(q, k, v, kmean, b, s):
    nq = s // TQ
    qspec, kvspec = _attn_specs(nq, TQ, s)
    return pl.pallas_call(
        functools.partial(_moba_kernel, slopes=_alibi_slopes(0), tq=TQ),
        grid=(b, nq),
        in_specs=[qspec, kvspec, kvspec, pl.BlockSpec((1, SEL_ROWS, GROUP_WIDTH), lambda bb, i: (bb, 0, 0))],
        out_specs=qspec,
        out_shape=jax.ShapeDtypeStruct(q.shape, BF16),
        scratch_shapes=_flash_scratch(TQ),
        compiler_params=_params("parallel", "arbitrary"),
        name="moba",
    )(q, k.reshape(b, s, GROUP_WIDTH), v.reshape(b, s, GROUP_WIDTH), kmean)


def _nsa_cmp_kernel(x_ref, pa_ref, pb_ref, w1a_ref, w1b_ref, b1_ref, w2_ref, o_ref):
    x = x_ref[0]
    n = x.shape[0]
    a = _mm(x + pa_ref[...], w1a_ref[...], precision=HIGHEST)
    bm = _mm(x + pb_ref[...], w1b_ref[...], precision=HIGHEST)
    hid = jax.nn.gelu(a + pltpu.roll(bm, n - 1, 0) + b1_ref[...], approximate=True)
    o_ref[0] = _mm(hid, w2_ref[...], precision=HIGHEST)


def _nsa_compress(x, pos, w1, b1, w2, b, s):
    rows = s // NSA_CMP_STRIDE
    half = NSA_CMP_STRIDE * HEAD_DIM
    hid = w1.shape[-1]
    full = lambda shape: pl.BlockSpec(shape, lambda bb: (0,) * len(shape))
    return pl.pallas_call(
        _nsa_cmp_kernel,
        grid=(b,),
        in_specs=[pl.BlockSpec((1, rows, half), lambda bb: (bb, 0, 0)),
                  full((1, half)), full((1, half)), full((half, hid)), full((half, hid)),
                  full((1, hid)), full((hid, GROUP_WIDTH))],
        out_specs=pl.BlockSpec((1, rows, GROUP_WIDTH), lambda bb: (bb, 0, 0)),
        out_shape=jax.ShapeDtypeStruct((b, rows, GROUP_WIDTH), F32),
        compiler_params=_params("parallel"),
        name="nsa_compress",
    )(x.reshape(b, rows, half),
      pos[:NSA_CMP_STRIDE].reshape(1, half), pos[NSA_CMP_STRIDE:].reshape(1, half),
      w1[:NSA_CMP_STRIDE].reshape(half, hid), w1[NSA_CMP_STRIDE:].reshape(half, hid),
      b1.reshape(1, hid), jnp.tile(w2, (1, GROUP_HEADS)))


def _nsa_sel_kernel(q_ref, kc_ref, vc_ref, ov_ref, oc_ref, sel_ref, *, slopes, tq):
    i = pl.program_id(1)
    scale = HEAD_DIM ** -0.5
    nc = kc_ref.shape[1]
    qf = q_ref[...].astype(F32)
    kc = kc_ref[0].astype(BF16)
    vc = vc_ref[0].astype(BF16)
    t = i * tq + lax.broadcasted_iota(jnp.int32, (tq, nc), 0)
    cmp_end = lax.broadcasted_iota(jnp.int32, (tq, nc), 1) * NSA_CMP_STRIDE + (NSA_CMP_LEN - 1)
    dist = (t - cmp_end).astype(F32)
    mask = cmp_end <= t
    p_sum = jnp.zeros((tq, nc), F32)
    for h in range(GROUP_HEADS):
        qs = (jnp.where(_head_mask(h), qf, 0.0) * scale).astype(BF16)
        sc = _nt(qs, kc) - slopes[h] * dist
        sc = jnp.where(mask, sc, NEG_INF)
        m = jnp.max(sc, axis=-1, keepdims=True)
        p = jnp.where(mask, jnp.exp(sc - m), 0.0)
        p = p / jnp.maximum(jnp.sum(p, axis=-1, keepdims=True), TINY)
        _store_head(oc_ref, h, _mm(p.astype(BF16), vc))
        p_sum = p_sum + p
    imp_t = _nt(ov_ref[...], p_sum, precision=HIGHEST)
    blk = lax.broadcasted_iota(jnp.int32, (SEL_ROWS, tq), 0)
    tt = i * tq + lax.broadcasted_iota(jnp.int32, (SEL_ROWS, tq), 1)
    qblk = lax.shift_right_logical(tt, 6)
    forced = (blk == 0) | (blk == qblk) | (blk == qblk - 1)
    score = jnp.where(forced, FORCED_SCORE, imp_t)
    score = jnp.where(blk * NSA_SEL_BLOCK <= tt, score, NEG_INF)
    sel_ref[...] = _topk_bias_t(score, NSA_SEL_TOPN).T.astype(BF16)


def _nsa_select(q, kc, vc, b, s):
    nq = s // TQ
    nc = s // NSA_CMP_STRIDE
    nsel = s // NSA_SEL_BLOCK
    assert nsel <= SEL_ROWS and NSA_SEL_TOPN <= nsel
    cs = np.arange(nc)[None, :] * NSA_CMP_STRIDE
    bs = np.arange(SEL_ROWS)[:, None] * NSA_SEL_BLOCK
    ov = np.clip(np.minimum(cs + NSA_CMP_LEN, bs + NSA_SEL_BLOCK) - np.maximum(cs, bs), 0, None) / NSA_CMP_LEN
    ov[:, nc - 1] = 0.0
    ov[nsel:] = 0.0
    qspec = pl.BlockSpec((TQ, GROUP_WIDTH), lambda bb, i: (bb * nq + i, 0))
    cspec = pl.BlockSpec((1, nc, GROUP_WIDTH), lambda bb, i: (bb, 0, 0))
    return pl.pallas_call(
        functools.partial(_nsa_sel_kernel, slopes=_alibi_slopes(1), tq=TQ),
        grid=(b, nq),
        in_specs=[qspec, cspec, cspec, pl.BlockSpec((SEL_ROWS, nc), lambda bb, i: (0, 0))],
        out_specs=[qspec, pl.BlockSpec((TQ, SEL_ROWS), lambda bb, i: (bb * nq + i, 0))],
        out_shape=[jax.ShapeDtypeStruct(q.shape, F32), jax.ShapeDtypeStruct((q.shape[0], SEL_ROWS), BF16)],
        compiler_params=_params("parallel", "arbitrary"),
        name="nsa_select",
    )(q, kc, vc, jnp.asarray(ov.astype(np.float32)))


def _nsa_attn_kernel(q_ref, ks_ref, vs_ref, kw_ref, vw_ref, sel_ref, oc_ref, g_ref, o_ref,
                     m_ref, l_ref, acc_ref, m2_ref, l2_ref, acc2_ref, *, slopes, tq):
    i = pl.program_id(1)
    scale = HEAD_DIM ** -0.5
    qf = q_ref[...].astype(F32)
    selbias = sel_ref[...]
    r, c = _rel_pos(tq)
    rc = (r - c).astype(F32)
    per_tile = tq // NSA_SEL_BLOCK
    diag_bias = _expand_bias(selbias, i * per_tile, per_tile, tq)
    for h in range(GROUP_HEADS):
        qs = (jnp.where(_head_mask(h), qf, 0.0) * scale).astype(BF16)
        table = rc * (-slopes[h])
        _flash_init(m_ref, l_ref, acc_ref)
        u = _nt(qs, _kv_tile(ks_ref, i, tq)) + table + diag_bias
        _flash_update(u, c <= r, 0.0, _kv_tile(vs_ref, i, tq), m_ref, l_ref, acc_ref)

        def body(j, carry, qs=qs, table=table, h=h):
            u = _nt(qs, _kv_tile(ks_ref, j, tq)) + table + _expand_bias(selbias, j * per_tile, per_tile, tq)
            cj = (i - j).astype(F32) * (-slopes[h] * tq)
            _flash_update(u, None, cj, _kv_tile(vs_ref, j, tq), m_ref, l_ref, acc_ref)
            return carry

        lax.fori_loop(0, i, body, 0)
        _flash_init(m2_ref, l2_ref, acc2_ref)
        u = _nt(qs, _kv_tile(kw_ref, i, tq)) + table
        _flash_update(u, c <= r, 0.0, _kv_tile(vw_ref, i, tq), m2_ref, l2_ref, acc2_ref)
        for back in range(1, NSA_WINDOW // tq + 1):
            @pl.when(i >= back)
            def _(back=back, qs=qs, table=table, h=h):
                u = _nt(qs, _kv_tile(kw_ref, i - back, tq)) + table
                mask = (c > r) if back * tq == NSA_WINDOW else None
                _flash_update(u, mask, -slopes[h] * tq * back, _kv_tile(vw_ref, i - back, tq),
                              m2_ref, l2_ref, acc2_ref)
        o_s = acc_ref[...] / l_ref[...]
        o_w = acc2_ref[...] / l2_ref[...]
        g = g_ref[...]
        out = (g[:, 0:GROUP_WIDTH] * oc_ref[...] + g[:, GROUP_WIDTH:2 * GROUP_WIDTH] * o_s
               + g[:, 2 * GROUP_WIDTH:3 * GROUP_WIDTH] * o_w)
        _store_head(o_ref, h, out)


def _nsa_attend(q, ks, vs, kw, vw, sel, o_c, gates, b, s):
    nq = s // TQ
    qspec, kvspec = _attn_specs(nq, TQ, s)
    row = lambda w: pl.BlockSpec((TQ, w), lambda bb, i: (bb * nq + i, 0))
    r3 = lambda a: a.reshape(b, s, GROUP_WIDTH)
    return pl.pallas_call(
        functools.partial(_nsa_attn_kernel, slopes=_alibi_slopes(1), tq=TQ),
        grid=(b, nq),
        in_specs=[qspec, kvspec, kvspec, kvspec, kvspec, row(SEL_ROWS), row(GROUP_WIDTH), row(3 * GROUP_WIDTH)],
        out_specs=qspec,
        out_shape=jax.ShapeDtypeStruct(q.shape, BF16),
        scratch_shapes=_flash_scratch(TQ, 2),
        compiler_params=_params("parallel", "arbitrary"),
        name="nsa_attend",
    )(q, r3(ks), r3(vs), r3(kw), r3(vw), sel, o_c, gates)


def _diff_kernel(q_ref, k_ref, v_ref, lam_ref, sub_ref, o_ref,
                 m1_ref, l1_ref, acc1_ref, m2_ref, l2_ref, acc2_ref, *, slopes, tq, lambda_init):
    i = pl.program_id(1)
    scale = DIFF_QK_DIM ** -0.5
    qf = q_ref[...].astype(F32)
    r, c = _rel_pos(tq)
    rc = (r - c).astype(F32)
    lam_p = lam_ref[...]
    lam = (jnp.exp(jnp.sum(lam_p[0:1] * lam_p[1:2], axis=-1, keepdims=True))
           - jnp.exp(jnp.sum(lam_p[2:3] * lam_p[3:4], axis=-1, keepdims=True)) + lambda_init)
    for h in range(GROUP_HEADS):
        q1 = (jnp.where(_head_mask(h, 0, DIFF_QK_DIM), qf, 0.0) * scale).astype(BF16)
        q2 = (jnp.where(_head_mask(h, DIFF_QK_DIM, HEAD_DIM), qf, 0.0) * scale).astype(BF16)
        table = rc * (-slopes[h])
        _flash_init(m1_ref, l1_ref, acc1_ref)
        _flash_init(m2_ref, l2_ref, acc2_ref)

        def step(j, mask, cj, q1=q1, q2=q2, table=table):
            kj = _kv_tile(k_ref, j, tq)
            vj = _kv_tile(v_ref, j, tq)
            _flash_update(_nt(q1, kj) + table, mask, cj, vj, m1_ref, l1_ref, acc1_ref)
            _flash_update(_nt(q2, kj) + table, mask, cj, vj, m2_ref, l2_ref, acc2_ref)

        step(i, c <= r, 0.0)

        def body(j, carry, step=step, h=h):
            step(j, None, (i - j).astype(F32) * (-slopes[h] * tq))
            return carry

        lax.fori_loop(0, i, body, 0)
        o = acc1_ref[...] / l1_ref[...] - lam * (acc2_ref[...] / l2_ref[...])
        ms = jnp.sum(jnp.where(_head_mask(h), o * o, 0.0), axis=-1, keepdims=True) * (1.0 / HEAD_DIM)
        y = ((o * lax.rsqrt(ms + NORM_EPS)) * sub_ref[...]) * (1.0 - lambda_init)
        _store_head(o_ref, h, y)


def _diff(q, k, v, lam, subln, lambda_init, b, s):
    nq = s // TQ
    qspec, kvspec = _attn_specs(nq, TQ, s)
    return pl.pallas_call(
        functools.partial(_diff_kernel, slopes=_alibi_slopes(2), tq=TQ, lambda_init=lambda_init),
        grid=(b, nq),
        in_specs=[qspec, kvspec, kvspec,
                  pl.BlockSpec((4, DIFF_QK_DIM), lambda bb, i: (0, 0)),
                  pl.BlockSpec((1, GROUP_WIDTH), lambda bb, i: (0, 0))],
        out_specs=qspec,
        out_shape=jax.ShapeDtypeStruct(q.shape, BF16),
        scratch_shapes=_flash_scratch(TQ, 2),
        compiler_params=_params("parallel", "arbitrary"),
        name="diff_attn",
    )(q, k.reshape(b, s, GROUP_WIDTH), v.reshape(b, s, GROUP_WIDTH), lam,
      jnp.tile(subln, GROUP_HEADS).reshape(1, GROUP_WIDTH))


def _swa_kernel(q_ref, k_ref, v_ref, sink_ref, o_ref, m_ref, l_ref, acc_ref, *, slopes, tq):
    i = pl.program_id(1)
    scale = HEAD_DIM ** -0.5
    qf = q_ref[...].astype(F32)
    r, c = _rel_pos(tq)
    d = r - c
    rc = d.astype(F32)
    for h in range(GROUP_HEADS):
        qs = (jnp.where(_head_mask(h), qf, 0.0) * scale).astype(BF16)
        table = rc * (-slopes[h])
        _flash_init(m_ref, l_ref, acc_ref)
        u = _nt(qs, _kv_tile(k_ref, i, tq)) + table
        _flash_update(u, (d >= 0) & (d < SWA_WINDOW), 0.0, _kv_tile(v_ref, i, tq), m_ref, l_ref, acc_ref)

        @pl.when(i >= 1)
        def _(qs=qs, table=table, h=h):
            u = _nt(qs, _kv_tile(k_ref, i - 1, tq)) + table
            _flash_update(u, d + tq < SWA_WINDOW, -slopes[h] * tq, _kv_tile(v_ref, i - 1, tq),
                          m_ref, l_ref, acc_ref)

        sink = sink_ref[0:1, h * HEAD_DIM:h * HEAD_DIM + 1]
        m_old = m_ref[...]
        m_new = jnp.maximum(m_old, sink)
        alpha = jnp.exp(m_old - m_new)
        denom = alpha * l_ref[...] + jnp.exp(sink - m_new)
        _store_head(o_ref, h, (alpha * acc_ref[...]) / denom)


def _swa(q, k, v, sinks, b, s):
    nq = s // TQ
    qspec, kvspec = _attn_specs(nq, TQ, s)
    return pl.pallas_call(
        functools.partial(_swa_kernel, slopes=_alibi_slopes(3), tq=TQ),
        grid=(b, nq),
        in_specs=[qspec, kvspec, kvspec, pl.BlockSpec((1, GROUP_WIDTH), lambda bb, i: (0, 0))],
        out_specs=qspec,
        out_shape=jax.ShapeDtypeStruct(q.shape, BF16),
        scratch_shapes=_flash_scratch(TQ),
        compiler_params=_params("parallel", "arbitrary"),
        name="swa",
    )(q, k.reshape(b, s, GROUP_WIDTH), v.reshape(b, s, GROUP_WIDTH),
      jnp.repeat(sinks, HEAD_DIM).reshape(1, GROUP_WIDTH))


def _out_proj_kernel(x_ref, a_ref, b_ref, c_ref, d_ref, w_ref, g_ref, o_ref):
    y = None
    for g, m_ref in enumerate((a_ref, b_ref, c_ref, d_ref)):
        part = _mm(m_ref[...], w_ref[g * GROUP_WIDTH:(g + 1) * GROUP_WIDTH, :])
        y = part if y is None else y + part
    o_ref[...] = x_ref[...] + _rms(y, g_ref[...])


def _out_proj(x2, mixes, w, gain):
    n, d = x2.shape
    tm = 512
    row = lambda wd: pl.BlockSpec((tm, wd), lambda i: (i, 0))
    return pl.pallas_call(
        _out_proj_kernel,
        grid=(n // tm,),
        in_specs=[row(d)] + [row(GROUP_WIDTH)] * 4 + [pl.BlockSpec((d, d), lambda i: (0, 0)),
                                                     pl.BlockSpec((1, d), lambda i: (0, 0))],
        out_specs=row(d),
        out_shape=jax.ShapeDtypeStruct((n, d), F32),
        compiler_params=_params("parallel"),
        name="out_proj",
    )(x2, *mixes, w, gain.reshape(1, d))


def _ffn_kernel(x_ref, xp_ref, gpre_ref, wg_ref, wu_ref, cw_ref, cb_ref, wd_ref, gpost_ref, o_ref,
                h_ref, hp_ref, acc_ref, *, tiles_per_seq):
    i = pl.program_id(0)
    f = pl.program_id(1)
    tm = x_ref.shape[0]

    @pl.when(f == 0)
    def _():
        h_ref[...] = _rms(x_ref[...], gpre_ref[...]).astype(BF16)
        hp = _rms(xp_ref[...], gpre_ref[...])
        hp_ref[...] = jnp.where(i % tiles_per_seq != 0, hp, 0.0).astype(BF16)
        acc_ref[...] = jnp.zeros(acc_ref.shape, F32)

    h = h_ref[...]
    a = _mm(h, wg_ref[...])
    ap = _mm(hp_ref[...], wg_ref[...])
    row = lax.broadcasted_iota(jnp.int32, a.shape, 0)
    a1 = jnp.where(row == 0, ap[7:8], pltpu.roll(a, 1, 0))
    a2 = jnp.where(row == 0, ap[6:7], jnp.where(row == 1, ap[7:8], pltpu.roll(a, 2, 0)))
    cw = cw_ref[...]
    conv = cw[0:1] * a2 + cw[1:2] * a1 + cw[2:3] * a + cb_ref[...]
    gated = jax.nn.gelu(conv, approximate=True) * _mm(h, wu_ref[...])
    acc_ref[...] += _mm(gated.astype(BF16), wd_ref[...])

    @pl.when(f == pl.num_programs(1) - 1)
    def _():
        o_ref[...] = x_ref[...] + _rms(acc_ref[...], gpost_ref[...])


def _ffn(x2, gpre, wg, wu, cw, cb, wd, gpost, s):
    n, d = x2.shape
    dff = wg.shape[1]
    tm, tf = 1024, 512
    halo = 8
    return pl.pallas_call(
        functools.partial(_ffn_kernel, tiles_per_seq=s // tm),
        grid=(n // tm, dff // tf),
        in_specs=[pl.BlockSpec((tm, d), lambda i, f: (i, 0)),
                  pl.BlockSpec((halo, d), lambda i, f: (jnp.maximum(i * (tm // halo) - 1, 0), 0)),
                  pl.BlockSpec((1, d), lambda i, f: (0, 0)),
                  pl.BlockSpec((d, tf), lambda i, f: (0, f)),
                  pl.BlockSpec((d, tf), lambda i, f: (0, f)),
                  pl.BlockSpec((CONV_WIDTH, tf), lambda i, f: (0, f)),
                  pl.BlockSpec((1, tf), lambda i, f: (0, f)),
                  pl.BlockSpec((tf, d), lambda i, f: (f, 0)),
                  pl.BlockSpec((1, d), lambda i, f: (0, 0))],
        out_specs=pl.BlockSpec((tm, d), lambda i, f: (i, 0)),
        out_shape=jax.ShapeDtypeStruct((n, d), F32),
        scratch_shapes=[pltpu.VMEM((tm, d), BF16), pltpu.VMEM((halo, d), BF16), pltpu.VMEM((tm, d), F32)],
        compiler_params=_params("parallel", "arbitrary"),
        name="conv_ffn",
    )(x2, x2, gpre.reshape(1, d), wg, wu, cw, cb.reshape(1, dff), wd, gpost.reshape(1, d))


def kernel(x, attn_pre_norm, attn_post_norm, ffn_pre_norm, ffn_post_norm, w_in, w_out, nsa_cmp_pos_k, nsa_cmp_w1_k, nsa_cmp_b1_k, nsa_cmp_w2_k, nsa_cmp_pos_v, nsa_cmp_w1_v, nsa_cmp_b1_v, nsa_cmp_w2_v, diff_lambda_q1, diff_lambda_k1, diff_lambda_q2, diff_lambda_k2, diff_subln, swa_sinks, ffn_w_gate, ffn_w_up, ffn_conv_w, ffn_conv_b, ffn_w_down):
    b, s, d = x.shape
    depth = w_in.shape[0]
    assert s % 1024 == 0 and s // MOBA_BLOCK <= SEL_ROWS
    cols = jnp.asarray(_in_columns())
    x2 = x.reshape(b * s, d)
    for l in range(depth):
        p = _in_proj(x2, attn_pre_norm[l], jnp.take(w_in[l], cols, axis=1).astype(BF16))
        kmean = p["kmean"].reshape(b, s // MOBA_BLOCK, GROUP_WIDTH)
        kmean = jnp.pad(kmean, ((0, 0), (0, SEL_ROWS - s // MOBA_BLOCK), (0, 0)))
        o_moba = _moba(p["mq"], p["mk"], p["mv"], kmean, b, s)
        kc = _nsa_compress(p["ncmp"][:, :HEAD_DIM], nsa_cmp_pos_k[l], nsa_cmp_w1_k[l], nsa_cmp_b1_k[l],
                           nsa_cmp_w2_k[l], b, s)
        vc = _nsa_compress(p["ncmp"][:, HEAD_DIM:], nsa_cmp_pos_v[l], nsa_cmp_w1_v[l], nsa_cmp_b1_v[l],
                           nsa_cmp_w2_v[l], b, s)
        o_cmp, sel = _nsa_select(p["nq"], kc, vc, b, s)
        o_nsa = _nsa_attend(p["nq"], p["nks"], p["nvs"], p["nkw"], p["nvw"], sel, o_cmp, p["gates"], b, s)
        lam = jnp.stack([diff_lambda_q1[l], diff_lambda_k1[l], diff_lambda_q2[l], diff_lambda_k2[l]])
        lambda_init = 0.8 - 0.6 * math.exp(-0.3 * l)
        o_diff = _diff(p["dq"], p["dk"], p["dv"], lam, diff_subln[l], lambda_init, b, s)
        o_swa = _swa(p["sq"], p["sk"], p["sv"], swa_sinks[l], b, s)
        x2 = _out_proj(x2, (o_moba, o_nsa, o_diff, o_swa), w_out[l].astype(BF16), attn_post_norm[l])
        x2 = _ffn(x2, ffn_pre_norm[l], ffn_w_gate[l].astype(BF16), ffn_w_up[l].astype(BF16), ffn_conv_w[l],
                  ffn_conv_b[l], ffn_w_down[l].astype(BF16), ffn_post_norm[l], s)
    return x2.reshape(b, s, d)
```

```python
import functools
import math

import numpy as np
import jax
import jax.numpy as jnp
from jax import lax
from jax.experimental import pallas as pl
from jax.experimental.pallas import tpu as pltpu

F32 = jnp.float32
BF16 = jnp.bfloat16
HIGHEST = lax.Precision.HIGHEST

N_MIXERS = 4
N_HEADS = 16
GROUP_HEADS = 4
HEAD_DIM = 64
GROUP_WIDTH = 256
MOBA_BLOCK = 256
MOBA_TOPK = 3
NSA_CMP_LEN = 32
NSA_CMP_STRIDE = 16
NSA_SEL_BLOCK = 64
NSA_SEL_TOPN = 16
NSA_WINDOW = 512
DIFF_QK_DIM = 32
SWA_KV_HEADS = 2
SWA_WINDOW = 128
CONV_WIDTH = 3
NORM_EPS = 1e-6
NEG_INF = -1e30
TINY = 1e-30
FORCED_SCORE = 1e4
REMOVED = -3e38

TQ = 256
SEL_ROWS = 128
VMEM_LIMIT = 56 * 1024 * 1024

AUG_BLK0 = HEAD_DIM
AUG_POS0 = AUG_BLK0 + SEL_ROWS
AUG_ONE0 = AUG_POS0 + 3


def _alibi_slopes(mixer):
    slopes = np.power(np.float32(2.0), np.arange(1, N_HEADS + 1, dtype=np.float32) * np.float32(-8.0 / N_HEADS))
    return [float(s) for s in slopes[mixer::N_MIXERS]]


def _bf16_pieces(x):
    out, rem = [], np.float32(x)
    for _ in range(3):
        p = np.float32(np.asarray(rem, np.float32).astype(jnp.bfloat16).astype(np.float32))
        out.append(float(p))
        rem = np.float32(rem - p)
    assert rem == 0.0
    return out


def _nt(a, b, precision=None):
    return lax.dot_general(a, b, (((1,), (1,)), ((), ())), preferred_element_type=F32, precision=precision)


def _mm(a, b, precision=None):
    return jnp.dot(a, b, preferred_element_type=F32, precision=precision)


def _rms(x, gain):
    return (x * lax.rsqrt(jnp.mean(x * x, axis=-1, keepdims=True) + NORM_EPS)) * gain


def _params(*sem):
    return pltpu.CompilerParams(dimension_semantics=sem, vmem_limit_bytes=VMEM_LIMIT)


_N_GROUPS = (("mk", 256, BF16), ("nks", 256, BF16), ("nkw", 128, BF16), ("dk", 256, BF16), ("sk", 128, BF16),
             ("ncmp", 128, F32))
_T_GROUPS = (("mq", 256, BF16), ("nq", 256, BF16), ("dq", 256, BF16), ("sq", 256, BF16),
             ("mv", 256, BF16), ("dv", 256, BF16), ("nvs", 64, BF16), ("nvw", 64, BF16), ("sv", 128, BF16),
             ("gates", 768, F32))
_ZERO_COL = 2700


def _in_columns():
    r = np.arange
    z = lambda n: np.full(n, _ZERO_COL)
    ncols = np.concatenate([r(256, 512), r(1152, 1216), z(192), r(1280, 1344), z(64), r(1676, 1932),
                            r(2444, 2572), r(1024, 1152)])
    gate = np.asarray([1408 + 3 * h + j for j in range(3) for h in range(4) for _ in range(64)])
    tcols = np.concatenate([r(0, 256), r(768, 1024), r(1420, 1676), r(2188, 2444), r(512, 768), r(1932, 2188),
                            r(1216, 1280), r(1344, 1408), r(2572, 2700), gate])
    assert ncols.shape[0] == sum(w for _, w, _ in _N_GROUPS) and tcols.shape[0] == sum(w for _, w, _ in _T_GROUPS)
    return ncols, tcols


def _in_proj_kernel(x_ref, g_ref, wn_ref, wt_ref, *out_refs, tm, seq):
    names = [n for n, _, _ in _N_GROUPS] + [n for n, _, _ in _T_GROUPS] + ["kmean"]
    outs = dict(zip(names, out_refs))
    hb = _rms(x_ref[...], g_ref[...]).astype(BF16)
    off = 0
    for name, width, dtype in _N_GROUPS:
        res = _mm(hb, wn_ref[:, off:off + width])
        off += width
        if name == "mk":
            for r in range(tm // MOBA_BLOCK):
                blk = res[r * MOBA_BLOCK:(r + 1) * MOBA_BLOCK]
                outs["kmean"][0, r:r + 1, :] = jnp.mean(blk, axis=0, keepdims=True)
        if name == "nks":
            lane = lax.broadcasted_iota(jnp.int32, res.shape, 1)
            row = lax.broadcasted_iota(jnp.int32, res.shape, 0)
            pos = (pl.program_id(0) % (seq // tm)) * tm + row
            onehot = (lane - AUG_BLK0) == lax.shift_right_logical(pos, 6)
            feat = jnp.where((lane >= AUG_POS0) & (lane < AUG_ONE0), (row % TQ).astype(F32),
                             jnp.where((lane >= AUG_ONE0) & (lane < AUG_ONE0 + 3), 1.0, 0.0))
            res = jnp.where(lane < AUG_BLK0, res, jnp.where(lane < AUG_POS0, jnp.where(onehot, 1.0, 0.0), feat))
        outs[name][...] = res.astype(dtype)
    off = 0
    for name, width, dtype in _T_GROUPS:
        res = _nt(wt_ref[off:off + width, :], hb)
        off += width
        if name == "gates":
            res = 1.0 / (1.0 + jnp.exp(-res))
        for c in range(tm // TQ):
            outs[name][c] = res[:, c * TQ:(c + 1) * TQ].astype(dtype)


def _in_proj(x2, gain, wn, wt, seq):
    n, d = x2.shape
    tm = 512
    out_shape = [jax.ShapeDtypeStruct((n, wd), dt) for _, wd, dt in _N_GROUPS]
    out_specs = [pl.BlockSpec((tm, wd), lambda i: (i, 0)) for _, wd, _ in _N_GROUPS]
    out_shape += [jax.ShapeDtypeStruct((n // TQ, wd, TQ), dt) for _, wd, dt in _T_GROUPS]
    out_specs += [pl.BlockSpec((tm // TQ, wd, TQ), lambda i: (i, 0, 0)) for _, wd, _ in _T_GROUPS]
    out_shape.append(jax.ShapeDtypeStruct((n // tm, tm // MOBA_BLOCK, 256), F32))
    out_specs.append(pl.BlockSpec((1, tm // MOBA_BLOCK, 256), lambda i: (i, 0, 0)))
    outs = pl.pallas_call(
        functools.partial(_in_proj_kernel, tm=tm, seq=seq),
        grid=(n // tm,),
        in_specs=[pl.BlockSpec((tm, d), lambda i: (i, 0)),
                  pl.BlockSpec((1, d), lambda i: (0, 0)),
                  pl.BlockSpec(wn.shape, lambda i: (0, 0)),
                  pl.BlockSpec(wt.shape, lambda i: (0, 0))],
        out_specs=out_specs,
        out_shape=out_shape,
        compiler_params=_params("parallel"),
        name="in_proj",
    )(x2, gain.reshape(1, d), wn, wt)
    names = [nm for nm, _, _ in _N_GROUPS] + [nm for nm, _, _ in _T_GROUPS] + ["kmean"]
    return dict(zip(names, outs))


def _rel_pos(tq):
    key = lax.broadcasted_iota(jnp.int32, (tq, tq), 0)
    qry = lax.broadcasted_iota(jnp.int32, (tq, tq), 1)
    return key, qry


def _head_rows(h, n=1):
    return slice(h * HEAD_DIM, (h + n) * HEAD_DIM)


def _flash_init(m_ref, l_ref, acc_ref):
    m_ref[...] = jnp.full(m_ref.shape, NEG_INF, F32)
    l_ref[...] = jnp.zeros(l_ref.shape, F32)
    acc_ref[...] = jnp.zeros(acc_ref.shape, F32)


def _flash_update(s_t, mask, cj, vt, idx, m_ref, l_ref, acc_ref):
    if mask is not None:
        s_t = jnp.where(mask, s_t, NEG_INF)
    m_old = m_ref[idx:idx + 1, :]
    m_new = jnp.maximum(m_old, jnp.max(s_t, axis=0, keepdims=True) + cj)
    p = jnp.exp(s_t - (m_new - cj))
    alpha = jnp.exp(m_old - m_new)
    l_ref[idx:idx + 1, :] = alpha * l_ref[idx:idx + 1, :] + jnp.sum(p, axis=0, keepdims=True)
    rows = _head_rows(idx)
    acc_ref[rows, :] = alpha * acc_ref[rows, :] + _mm(vt, p.astype(BF16))
    m_ref[idx:idx + 1, :] = m_new


def _topk_bias_t(score_t, k):
    nrow = score_t.shape[0]
    rows = lax.broadcasted_iota(jnp.int32, score_t.shape, 0).astype(F32)
    bias = jnp.full(score_t.shape, NEG_INF, F32)
    work = score_t
    for _ in range(k):
        m = jnp.max(work, axis=0, keepdims=True)
        idx = jnp.min(jnp.where(work == m, rows, float(nrow)), axis=0, keepdims=True)
        pick = rows == idx
        bias = jnp.where(pick, jnp.where(m > 0.5 * NEG_INF, 0.0, NEG_INF), bias)
        work = jnp.where(pick, REMOVED, work)
    return bias


def _k_tile(ref, j, tk):
    return ref[0, pl.ds(pl.multiple_of(j * tk, tk), tk), :]


def _qt_spec(width, nq):
    return pl.BlockSpec((1, width, TQ), lambda b, i: (b * nq + i, 0, 0))


def _k_spec(s, width):
    return pl.BlockSpec((1, s, width), lambda b, i: (b, 0, 0))


def _vt_spec(nq, width):
    return pl.BlockSpec((nq, width, TQ), lambda b, i: (b, 0, 0))


def _row_spec(width, nq):
    return pl.BlockSpec((TQ, width), lambda b, i: (b * nq + i, 0))


def _stat_scratch(acc_rows):
    return [pltpu.VMEM((8, TQ), F32), pltpu.VMEM((8, TQ), F32), pltpu.VMEM((acc_rows, TQ), F32)]


def _moba_kernel(qt_ref, k_ref, vt_ref, km_ref, o_ref, rhs_ref, tab_ref, sel_ref, m_ref, l_ref, acc_ref,
                 *, slopes, tq):
    i = pl.program_id(1)
    scale = HEAD_DIM ** -0.5
    qt = qt_ref[0].astype(F32)
    key, qry = _rel_pos(tq)
    rel = (qry - key).astype(F32)
    causal = key <= qry
    row_head = lax.broadcasted_iota(jnp.int32, (GROUP_WIDTH, 1), 0) // HEAD_DIM
    blk_rows = lax.broadcasted_iota(jnp.int32, (SEL_ROWS, tq), 0)
    _flash_init(m_ref, l_ref, acc_ref)
    kd = _k_tile(k_ref, i, tq)
    for h in range(GROUP_HEADS):
        qh = jnp.where(row_head == h, qt, 0.0)
        gate_t = _mm(km_ref[0], qh, precision=HIGHEST)
        gate_t = jnp.where(blk_rows < i, gate_t, NEG_INF)
        sel_ref[h] = _topk_bias_t(gate_t, MOBA_TOPK)
        rhs_ref[h] = (qh * scale).astype(BF16)
        tab_ref[h] = rel * (-slopes[h])
        s_t = _mm(kd, rhs_ref[h]) + tab_ref[h]
        _flash_update(s_t, causal, 0.0, vt_ref[i, _head_rows(h), :], h, m_ref, l_ref, acc_ref)

    def body(j, carry):
        kj = _k_tile(k_ref, j, tq)
        dj = (i - j).astype(F32)
        for h in range(GROUP_HEADS):
            s_t = _mm(kj, rhs_ref[h]) + tab_ref[h]
            cj = dj * (-slopes[h] * tq) + sel_ref[h, pl.ds(j, 1), :]
            _flash_update(s_t, None, cj, vt_ref[j, _head_rows(h), :], h, m_ref, l_ref, acc_ref)
        return carry

    lax.fori_loop(0, i, body, 0)
    for h in range(GROUP_HEADS):
        acc_ref[_head_rows(h), :] = acc_ref[_head_rows(h), :] / l_ref[h:h + 1, :]
    o_ref[...] = acc_ref[...].T.astype(o_ref.dtype)


def _moba(qt, k, vt, kmean, b, s):
    nq = s // TQ
    return pl.pallas_call(
        functools.partial(_moba_kernel, slopes=_alibi_slopes(0), tq=TQ),
        grid=(b, nq),
        in_specs=[_qt_spec(GROUP_WIDTH, nq), _k_spec(s, GROUP_WIDTH), _vt_spec(nq, GROUP_WIDTH),
                  pl.BlockSpec((1, SEL_ROWS, GROUP_WIDTH), lambda bb, i: (bb, 0, 0))],
        out_specs=_row_spec(GROUP_WIDTH, nq),
        out_shape=jax.ShapeDtypeStruct((b * s, GROUP_WIDTH), BF16),
        scratch_shapes=[pltpu.VMEM((GROUP_HEADS, GROUP_WIDTH, TQ), BF16),
                        pltpu.VMEM((GROUP_HEADS, TQ, TQ), F32),
                        pltpu.VMEM((GROUP_HEADS, SEL_ROWS, TQ), F32)] + _stat_scratch(GROUP_WIDTH),
        compiler_params=_params("parallel", "arbitrary"),
        name="moba",
    )(qt, k.reshape(b, s, GROUP_WIDTH), vt, kmean)


def _nsa_cmp_kernel(x_ref, pa_ref, pb_ref, w1a_ref, w1b_ref, b1_ref, w2_ref, o_ref, *, transposed):
    x = x_ref[0]
    n = x.shape[0]
    a = _mm(x + pa_ref[...], w1a_ref[...], precision=HIGHEST)
    bm = _mm(x + pb_ref[...], w1b_ref[...], precision=HIGHEST)
    hid = jax.nn.gelu(a + pltpu.roll(bm, n - 1, 0) + b1_ref[...], approximate=True)
    if transposed:
        o_ref[0] = _nt(w2_ref[...], hid, precision=HIGHEST)
    else:
        o_ref[0] = _mm(hid, w2_ref[...], precision=HIGHEST)


def _nsa_compress(x, pos, w1, b1, w2, b, s, transposed):
    rows = s // NSA_CMP_STRIDE
    half = NSA_CMP_STRIDE * HEAD_DIM
    hid = w1.shape[-1]
    full = lambda shape: pl.BlockSpec(shape, lambda bb: (0,) * len(shape))
    if transposed:
        w2 = w2.T
        oshape = (b, HEAD_DIM, rows)
    else:
        w2 = jnp.pad(w2, ((0, 0), (0, 128 - HEAD_DIM)))
        oshape = (b, rows, 128)
    return pl.pallas_call(
        functools.partial(_nsa_cmp_kernel, transposed=transposed),
        grid=(b,),
        in_specs=[pl.BlockSpec((1, rows, half), lambda bb: (bb, 0, 0)),
                  full((1, half)), full((1, half)), full((half, hid)), full((half, hid)),
                  full((1, hid)), full(w2.shape)],
        out_specs=pl.BlockSpec((1,) + oshape[1:], lambda bb: (bb, 0, 0)),
        out_shape=jax.ShapeDtypeStruct(oshape, F32),
        compiler_params=_params("parallel"),
        name="nsa_compress",
    )(x.reshape(b, rows, half),
      pos[:NSA_CMP_STRIDE].reshape(1, half), pos[NSA_CMP_STRIDE:].reshape(1, half),
      w1[:NSA_CMP_STRIDE].reshape(half, hid), w1[NSA_CMP_STRIDE:].reshape(half, hid),
      b1.reshape(1, hid), w2)


def _nsa_sel_kernel(qt_ref, kc_ref, vct_ref, ov_ref, oc_ref, sel_ref, *, slopes, tq):
    i = pl.program_id(1)
    scale = HEAD_DIM ** -0.5
    nc = kc_ref.shape[1]
    qt = qt_ref[0].astype(F32)
    kc = kc_ref[0].astype(BF16)
    vct = vct_ref[0].astype(BF16)
    t = i * tq + lax.broadcasted_iota(jnp.int32, (nc, tq), 1)
    cmp_end = lax.broadcasted_iota(jnp.int32, (nc, tq), 0) * NSA_CMP_STRIDE + (NSA_CMP_LEN - 1)
    dist = (t - cmp_end).astype(F32)
    mask = cmp_end <= t
    zeros = jnp.zeros((HEAD_DIM, tq), BF16)
    p_sum = jnp.zeros((nc, tq), F32)
    for h in range(GROUP_HEADS):
        qs = (qt[_head_rows(h), :] * scale).astype(BF16)
        sc = _mm(kc, jnp.concatenate([qs, zeros], axis=0)) - slopes[h] * dist
        sc = jnp.where(mask, sc, NEG_INF)
        m = jnp.max(sc, axis=0, keepdims=True)
        p = jnp.where(mask, jnp.exp(sc - m), 0.0)
        p = p / jnp.maximum(jnp.sum(p, axis=0, keepdims=True), TINY)
        oc_ref[0, _head_rows(h), :] = _mm(vct, p.astype(BF16))
        p_sum = p_sum + p
    imp_t = _mm(ov_ref[...], p_sum, precision=HIGHEST)
    blk = lax.broadcasted_iota(jnp.int32, (SEL_ROWS, tq), 0)
    tt = i * tq + lax.broadcasted_iota(jnp.int32, (SEL_ROWS, tq), 1)
    qblk = lax.shift_right_logical(tt, 6)
    forced = (blk == 0) | (blk == qblk) | (blk == qblk - 1)
    score = jnp.where(forced, FORCED_SCORE, imp_t)
    score = jnp.where(blk * NSA_SEL_BLOCK <= tt, score, NEG_INF)
    sel_ref[0] = _topk_bias_t(score, NSA_SEL_TOPN).astype(sel_ref.dtype)


def _nsa_select(qt, kc, vct, b, s):
    nq = s // TQ
    nc = s // NSA_CMP_STRIDE
    nsel = s // NSA_SEL_BLOCK
    assert nsel <= SEL_ROWS and NSA_SEL_TOPN <= nsel
    cs = np.arange(nc)[None, :] * NSA_CMP_STRIDE
    bs = np.arange(SEL_ROWS)[:, None] * NSA_SEL_BLOCK
    ov = np.clip(np.minimum(cs + NSA_CMP_LEN, bs + NSA_SEL_BLOCK) - np.maximum(cs, bs), 0, None) / NSA_CMP_LEN
    ov[:, nc - 1] = 0.0
    ov[nsel:] = 0.0
    return pl.pallas_call(
        functools.partial(_nsa_sel_kernel, slopes=_alibi_slopes(1), tq=TQ),
        grid=(b, nq),
        in_specs=[_qt_spec(GROUP_WIDTH, nq),
                  pl.BlockSpec((1, nc, 128), lambda bb, i: (bb, 0, 0)),
                  pl.BlockSpec((1, HEAD_DIM, nc), lambda bb, i: (bb, 0, 0)),
                  pl.BlockSpec((SEL_ROWS, nc), lambda bb, i: (0, 0))],
        out_specs=[_qt_spec(GROUP_WIDTH, nq), _qt_spec(SEL_ROWS, nq)],
        out_shape=[jax.ShapeDtypeStruct((b * nq, GROUP_WIDTH, TQ), F32),
                   jax.ShapeDtypeStruct((b * nq, SEL_ROWS, TQ), BF16)],
        compiler_params=_params("parallel", "arbitrary"),
        name="nsa_select",
    )(qt, kc, vct, jnp.asarray(ov.astype(np.float32)))


def _nsa_attn_kernel(qt_ref, ks_ref, vst_ref, kw_ref, vwt_ref, sel_ref, oc_ref, g_ref, o_ref,
                     rhs_ref, rhsw_ref, m_ref, l_ref, acc_ref, *, slopes, tq):
    i = pl.program_id(1)
    scale = HEAD_DIM ** -0.5
    nh = GROUP_HEADS
    qt = qt_ref[0].astype(F32)
    key, qry = _rel_pos(tq)
    rel = (qry - key).astype(F32)
    causal = key <= qry
    frow = lax.broadcasted_iota(jnp.int32, (HEAD_DIM, tq), 0)
    qoff = lax.broadcasted_iota(jnp.int32, (1, tq), 1).astype(F32)
    zeros = jnp.zeros((HEAD_DIM, tq), BF16)
    _flash_init(m_ref, l_ref, acc_ref)
    ksd = _k_tile(ks_ref, i, tq)
    kwd = _k_tile(kw_ref, i, tq)
    for h in range(nh):
        qs = (qt[_head_rows(h), :] * scale).astype(BF16)
        s1, s2, s3 = _bf16_pieces(slopes[h])
        t = qoff * (-slopes[h])
        t1 = t.astype(BF16).astype(F32)
        t2 = (t - t1).astype(BF16).astype(F32)
        t3 = t - t1 - t2
        feat = jnp.where(frow == 0, s1, jnp.where(frow == 1, s2, jnp.where(frow == 2, s3, jnp.where(
            frow == 3, t1, jnp.where(frow == 4, t2, jnp.where(frow == 5, t3, 0.0))))))
        rhs_ref[h, 0:AUG_BLK0, :] = qs
        rhs_ref[h, AUG_BLK0:AUG_POS0, :] = sel_ref[0]
        rhs_ref[h, AUG_POS0:GROUP_WIDTH, :] = feat.astype(BF16)
        rhsw_ref[h] = jnp.concatenate([qs, zeros], axis=0)
        _flash_update(_mm(ksd, rhs_ref[h]), causal, 0.0, vst_ref[i], h, m_ref, l_ref, acc_ref)
        _flash_update(_mm(kwd, rhsw_ref[h]) + rel * (-slopes[h]), causal, 0.0, vwt_ref[i], nh + h,
                      m_ref, l_ref, acc_ref)

    def body(j, carry):
        ksj = _k_tile(ks_ref, j, tq)
        dj = (i - j).astype(F32)
        for h in range(nh):
            _flash_update(_mm(ksj, rhs_ref[h]), None, dj * (-slopes[h] * tq), vst_ref[j], h, m_ref, l_ref, acc_ref)
        return carry

    lax.fori_loop(0, i, body, 0)

    for back in range(1, NSA_WINDOW // tq + 1):
        @pl.when(i >= back)
        def _(back=back):
            kwj = _k_tile(kw_ref, i - back, tq)
            mask = (key > qry) if back * tq == NSA_WINDOW else None
            for h in range(nh):
                s_t = _mm(kwj, rhsw_ref[h]) + rel * (-slopes[h])
                _flash_update(s_t, mask, -slopes[h] * tq * back, vwt_ref[i - back], nh + h, m_ref, l_ref, acc_ref)

    for h in range(nh):
        rows = _head_rows(h)
        o_s = acc_ref[rows, :] / l_ref[h:h + 1, :]
        o_w = acc_ref[_head_rows(nh + h), :] / l_ref[nh + h:nh + h + 1, :]
        acc_ref[rows, :] = (g_ref[0, rows, :] * oc_ref[0, rows, :]
                            + g_ref[0, _head_rows(nh + h), :] * o_s
                            + g_ref[0, _head_rows(2 * nh + h), :] * o_w)
    o_ref[...] = acc_ref[0:GROUP_WIDTH, :].T.astype(o_ref.dtype)


def _nsa_attend(qt, ks, vst, kw, vwt, sel, oct, gt, b, s):
    nq = s // TQ
    return pl.pallas_call(
        functools.partial(_nsa_attn_kernel, slopes=_alibi_slopes(1), tq=TQ),
        grid=(b, nq),
        in_specs=[_qt_spec(GROUP_WIDTH, nq), _k_spec(s, GROUP_WIDTH), _vt_spec(nq, HEAD_DIM),
                  _k_spec(s, 128), _vt_spec(nq, HEAD_DIM), _qt_spec(SEL_ROWS, nq), _qt_spec(GROUP_WIDTH, nq),
                  _qt_spec(3 * GROUP_WIDTH, nq)],
        out_specs=_row_spec(GROUP_WIDTH, nq),
        out_shape=jax.ShapeDtypeStruct((b * s, GROUP_WIDTH), BF16),
        scratch_shapes=[pltpu.VMEM((GROUP_HEADS, GROUP_WIDTH, TQ), BF16),
                        pltpu.VMEM((GROUP_HEADS, 128, TQ), BF16)] + _stat_scratch(2 * GROUP_WIDTH),
        compiler_params=_params("parallel", "arbitrary"),
        name="nsa_attend",
    )(qt, ks.reshape(b, s, GROUP_WIDTH), vst, kw.reshape(b, s, 128), vwt, sel, oct, gt)


def _diff_kernel(qt_ref, k_ref, vt_ref, lam_ref, sub_ref, o_ref, rhs_ref, tab_ref, m_ref, l_ref, acc_ref,
                 *, slopes, tq, lambda_init):
    i = pl.program_id(1)
    scale = DIFF_QK_DIM ** -0.5
    nh = GROUP_HEADS
    qt = qt_ref[0].astype(F32)
    key, qry = _rel_pos(tq)
    rel = (qry - key).astype(F32)
    causal = key <= qry
    row = lax.broadcasted_iota(jnp.int32, (GROUP_WIDTH, 1), 0)
    _flash_init(m_ref, l_ref, acc_ref)
    kd = _k_tile(k_ref, i, tq)
    for h in range(nh):
        lo = h * HEAD_DIM
        rhs_ref[h] = (jnp.where((row >= lo) & (row < lo + DIFF_QK_DIM), qt, 0.0) * scale).astype(BF16)
        rhs_ref[nh + h] = (jnp.where((row >= lo + DIFF_QK_DIM) & (row < lo + HEAD_DIM), qt, 0.0)
                           * scale).astype(BF16)
        tab_ref[h] = rel * (-slopes[h])
        vd = vt_ref[i, _head_rows(h), :]
        _flash_update(_mm(kd, rhs_ref[h]) + tab_ref[h], causal, 0.0, vd, h, m_ref, l_ref, acc_ref)
        _flash_update(_mm(kd, rhs_ref[nh + h]) + tab_ref[h], causal, 0.0, vd, nh + h, m_ref, l_ref, acc_ref)

    def body(j, carry):
        kj = _k_tile(k_ref, j, tq)
        dj = (i - j).astype(F32)
        for h in range(nh):
            cj = dj * (-slopes[h] * tq)
            vj = vt_ref[j, _head_rows(h), :]
            _flash_update(_mm(kj, rhs_ref[h]) + tab_ref[h], None, cj, vj, h, m_ref, l_ref, acc_ref)
            _flash_update(_mm(kj, rhs_ref[nh + h]) + tab_ref[h], None, cj, vj, nh + h, m_ref, l_ref, acc_ref)
        return carry

    lax.fori_loop(0, i, body, 0)
    lam_p = lam_ref[...]
    lam = (jnp.exp(jnp.sum(lam_p[0:1] * lam_p[1:2], axis=-1, keepdims=True))
           - jnp.exp(jnp.sum(lam_p[2:3] * lam_p[3:4], axis=-1, keepdims=True)) + lambda_init)
    for h in range(nh):
        rows = _head_rows(h)
        o = (acc_ref[rows, :] / l_ref[h:h + 1, :]
             - lam * (acc_ref[_head_rows(nh + h), :] / l_ref[nh + h:nh + h + 1, :]))
        ms = jnp.mean(o * o, axis=0, keepdims=True)
        acc_ref[rows, :] = ((o * lax.rsqrt(ms + NORM_EPS)) * sub_ref[...]) * (1.0 - lambda_init)
    o_ref[...] = acc_ref[0:GROUP_WIDTH, :].T.astype(o_ref.dtype)


def _diff(qt, k, vt, lam, subln, lambda_init, b, s):
    nq = s // TQ
    return pl.pallas_call(
        functools.partial(_diff_kernel, slopes=_alibi_slopes(2), tq=TQ, lambda_init=lambda_init),
        grid=(b, nq),
        in_specs=[_qt_spec(GROUP_WIDTH, nq), _k_spec(s, GROUP_WIDTH), _vt_spec(nq, GROUP_WIDTH),
                  pl.BlockSpec((4, DIFF_QK_DIM), lambda bb, i: (0, 0)),
                  pl.BlockSpec((HEAD_DIM, TQ), lambda bb, i: (0, 0))],
        out_specs=_row_spec(GROUP_WIDTH, nq),
        out_shape=jax.ShapeDtypeStruct((b * s, GROUP_WIDTH), BF16),
        scratch_shapes=[pltpu.VMEM((2 * GROUP_HEADS, GROUP_WIDTH, TQ), BF16),
                        pltpu.VMEM((GROUP_HEADS, TQ, TQ), F32)] + _stat_scratch(2 * GROUP_WIDTH),
        compiler_params=_params("parallel", "arbitrary"),
        name="diff_attn",
    )(qt, k.reshape(b, s, GROUP_WIDTH), vt, lam, jnp.broadcast_to(subln[:, None], (HEAD_DIM, TQ)))


def _swa_kernel(qt_ref, k_ref, vt_ref, sink_ref, o_ref, rhs_ref, m_ref, l_ref, acc_ref, *, slopes, tq):
    i = pl.program_id(1)
    scale = HEAD_DIM ** -0.5
    nh = GROUP_HEADS
    per_kv = nh // SWA_KV_HEADS
    qt = qt_ref[0].astype(F32)
    key, qry = _rel_pos(tq)
    d = qry - key
    rel = d.astype(F32)
    zeros = jnp.zeros((HEAD_DIM, tq), BF16)
    _flash_init(m_ref, l_ref, acc_ref)
    kd = _k_tile(k_ref, i, tq)
    for h in range(nh):
        g = h // per_kv
        qs = (qt[_head_rows(h), :] * scale).astype(BF16)
        rhs_ref[h] = jnp.concatenate([qs, zeros] if g == 0 else [zeros, qs], axis=0)
        s_t = _mm(kd, rhs_ref[h]) + rel * (-slopes[h])
        _flash_update(s_t, (d >= 0) & (d < SWA_WINDOW), 0.0, vt_ref[i, _head_rows(g), :], h, m_ref, l_ref, acc_ref)

    @pl.when(i >= 1)
    def _():
        kj = _k_tile(k_ref, i - 1, tq)
        for h in range(nh):
            g = h // per_kv
            s_t = _mm(kj, rhs_ref[h]) + rel * (-slopes[h])
            _flash_update(s_t, d + tq < SWA_WINDOW, -slopes[h] * tq, vt_ref[i - 1, _head_rows(g), :], h,
                          m_ref, l_ref, acc_ref)

    for h in range(nh):
        sink = sink_ref[h:h + 1, 0:1]
        m_old = m_ref[h:h + 1, :]
        m_new = jnp.maximum(m_old, sink)
        alpha = jnp.exp(m_old - m_new)
        denom = alpha * l_ref[h:h + 1, :] + jnp.exp(sink - m_new)
        acc_ref[_head_rows(h), :] = (alpha * acc_ref[_head_rows(h), :]) / denom
    o_ref[...] = acc_ref[...].T.astype(o_ref.dtype)


def _swa(qt, k, vt, sinks, b, s):
    nq = s // TQ
    kvw = SWA_KV_HEADS * HEAD_DIM
    return pl.pallas_call(
        functools.partial(_swa_kernel, slopes=_alibi_slopes(3), tq=TQ),
        grid=(b, nq),
        in_specs=[_qt_spec(GROUP_WIDTH, nq), _k_spec(s, kvw), _vt_spec(nq, kvw),
                  pl.BlockSpec((8, 128), lambda bb, i: (0, 0))],
        out_specs=_row_spec(GROUP_WIDTH, nq),
        out_shape=jax.ShapeDtypeStruct((b * s, GROUP_WIDTH), BF16),
        scratch_shapes=[pltpu.VMEM((GROUP_HEADS, kvw, TQ), BF16)] + _stat_scratch(GROUP_WIDTH),
        compiler_params=_params("parallel", "arbitrary"),
        name="swa",
    )(qt, k.reshape(b, s, kvw), vt, jnp.pad(jnp.broadcast_to(sinks[:, None], (GROUP_HEADS, 128)), ((0, 4), (0, 0))))


def _out_proj_kernel(x_ref, a_ref, b_ref, c_ref, d_ref, w_ref, g_ref, o_ref):
    y = None
    for g, m_ref in enumerate((a_ref, b_ref, c_ref, d_ref)):
        part = _mm(m_ref[...], w_ref[g * GROUP_WIDTH:(g + 1) * GROUP_WIDTH, :])
        y = part if y is None else y + part
    o_ref[...] = x_ref[...] + _rms(y, g_ref[...])


def _out_proj(x2, mixes, w, gain):
    n, d = x2.shape
    tm = 512
    row = lambda wd: pl.BlockSpec((tm, wd), lambda i: (i, 0))
    return pl.pallas_call(
        _out_proj_kernel,
        grid=(n // tm,),
        in_specs=[row(d)] + [row(GROUP_WIDTH)] * 4 + [pl.BlockSpec((d, d), lambda i: (0, 0)),
                                                     pl.BlockSpec((1, d), lambda i: (0, 0))],
        out_specs=row(d),
        out_shape=jax.ShapeDtypeStruct((n, d), F32),
        compiler_params=_params("parallel"),
        name="out_proj",
    )(x2, *mixes, w, gain.reshape(1, d))


def _ffn_kernel(x_ref, xp_ref, gpre_ref, wg_ref, wu_ref, cw_ref, cb_ref, wd_ref, gpost_ref, o_ref,
                h_ref, hp_ref, acc_ref, *, tiles_per_seq):
    i = pl.program_id(0)
    f = pl.program_id(1)

    @pl.when(f == 0)
    def _():
        h_ref[...] = _rms(x_ref[...], gpre_ref[...]).astype(BF16)
        hp = _rms(xp_ref[...], gpre_ref[...])
        hp_ref[...] = jnp.where(i % tiles_per_seq != 0, hp, 0.0).astype(BF16)
        acc_ref[...] = jnp.zeros(acc_ref.shape, F32)

    h = h_ref[...]
    a = _mm(h, wg_ref[...])
    ap = _mm(hp_ref[...], wg_ref[...])
    row = lax.broadcasted_iota(jnp.int32, a.shape, 0)
    a1 = jnp.where(row == 0, ap[7:8], pltpu.roll(a, 1, 0))
    a2 = jnp.where(row == 0, ap[6:7], jnp.where(row == 1, ap[7:8], pltpu.roll(a, 2, 0)))
    cw = cw_ref[...]
    conv = cw[0:1] * a2 + cw[1:2] * a1 + cw[2:3] * a + cb_ref[...]
    gated = jax.nn.gelu(conv, approximate=True) * _mm(h, wu_ref[...])
    acc_ref[...] += _mm(gated.astype(BF16), wd_ref[...])

    @pl.when(f == pl.num_programs(1) - 1)
    def _():
        o_ref[...] = x_ref[...] + _rms(acc_ref[...], gpost_ref[...])


def _ffn(x2, gpre, wg, wu, cw, cb, wd, gpost, s):
    n, d = x2.shape
    dff = wg.shape[1]
    tm, tf = 1024, 512
    halo = 8
    return pl.pallas_call(
        functools.partial(_ffn_kernel, tiles_per_seq=s // tm),
        grid=(n // tm, dff // tf),
        in_specs=[pl.BlockSpec((tm, d), lambda i, f: (i, 0)),
                  pl.BlockSpec((halo, d), lambda i, f: (jnp.maximum(i * (tm // halo) - 1, 0), 0)),
                  pl.BlockSpec((1, d), lambda i, f: (0, 0)),
                  pl.BlockSpec((d, tf), lambda i, f: (0, f)),
                  pl.BlockSpec((d, tf), lambda i, f: (0, f)),
                  pl.BlockSpec((CONV_WIDTH, tf), lambda i, f: (0, f)),
                  pl.BlockSpec((1, tf), lambda i, f: (0, f)),
                  pl.BlockSpec((tf, d), lambda i, f: (f, 0)),
                  pl.BlockSpec((1, d), lambda i, f: (0, 0))],
        out_specs=pl.BlockSpec((tm, d), lambda i, f: (i, 0)),
        out_shape=jax.ShapeDtypeStruct((n, d), F32),
        scratch_shapes=[pltpu.VMEM((tm, d), BF16), pltpu.VMEM((halo, d), BF16), pltpu.VMEM((tm, d), F32)],
        compiler_params=_params("parallel", "arbitrary"),
        name="conv_ffn",
    )(x2, x2, gpre.reshape(1, d), wg, wu, cw, cb.reshape(1, dff), wd, gpost.reshape(1, d))


def kernel(x, attn_pre_norm, attn_post_norm, ffn_pre_norm, ffn_post_norm, w_in, w_out, nsa_cmp_pos_k, nsa_cmp_w1_k, nsa_cmp_b1_k, nsa_cmp_w2_k, nsa_cmp_pos_v, nsa_cmp_w1_v, nsa_cmp_b1_v, nsa_cmp_w2_v, diff_lambda_q1, diff_lambda_k1, diff_lambda_q2, diff_lambda_k2, diff_subln, swa_sinks, ffn_w_gate, ffn_w_up, ffn_conv_w, ffn_conv_b, ffn_w_down):
    b, s, d = x.shape
    depth = w_in.shape[0]
    assert s % 1024 == 0 and s // MOBA_BLOCK <= SEL_ROWS
    ncols, tcols = (jnp.asarray(c) for c in _in_columns())
    x2 = x.reshape(b * s, d)
    for l in range(depth):
        w_ext = jnp.pad(w_in[l], ((0, 0), (0, 1)))
        p = _in_proj(x2, attn_pre_norm[l], jnp.take(w_ext, ncols, axis=1).astype(BF16),
                     jnp.take(w_ext, tcols, axis=1).T.astype(BF16), s)
        kmean = p["kmean"].reshape(b, s // MOBA_BLOCK, GROUP_WIDTH)
        kmean = jnp.pad(kmean, ((0, 0), (0, SEL_ROWS - s // MOBA_BLOCK), (0, 0)))
        o_moba = _moba(p["mq"], p["mk"], p["mv"], kmean, b, s)
        kc = _nsa_compress(p["ncmp"][:, :HEAD_DIM], nsa_cmp_pos_k[l], nsa_cmp_w1_k[l], nsa_cmp_b1_k[l],
                           nsa_cmp_w2_k[l], b, s, False)
        vct = _nsa_compress(p["ncmp"][:, HEAD_DIM:], nsa_cmp_pos_v[l], nsa_cmp_w1_v[l], nsa_cmp_b1_v[l],
                            nsa_cmp_w2_v[l], b, s, True)
        o_cmp, sel = _nsa_select(p["nq"], kc, vct, b, s)
        o_nsa = _nsa_attend(p["nq"], p["nks"], p["nvs"], p["nkw"], p["nvw"], sel, o_cmp, p["gates"], b, s)
        lam = jnp.stack([diff_lambda_q1[l], diff_lambda_k1[l], diff_lambda_q2[l], diff_lambda_k2[l]])
        lambda_init = 0.8 - 0.6 * math.exp(-0.3 * l)
        o_diff = _diff(p["dq"], p["dk"], p["dv"], lam, diff_subln[l], lambda_init, b, s)
        o_swa = _swa(p["sq"], p["sk"], p["sv"], swa_sinks[l], b, s)
        x2 = _out_proj(x2, (o_moba, o_nsa, o_diff, o_swa), w_out[l].astype(BF16), attn_post_norm[l])
        x2 = _ffn(x2, ffn_pre_norm[l], ffn_w_gate[l].astype(BF16), ffn_w_up[l].astype(BF16), ffn_conv_w[l],
                  ffn_conv_b[l], ffn_w_down[l].astype(BF16), ffn_post_norm[l], s)
    return x2.reshape(b, s, d)
```

```python
import functools
import math

import numpy as np
import jax
import jax.numpy as jnp
from jax import lax
from jax.experimental import pallas as pl
from jax.experimental.pallas import tpu as pltpu

F32 = jnp.float32
BF16 = jnp.bfloat16
HIGHEST = lax.Precision.HIGHEST

N_MIXERS = 4
N_HEADS = 16
GROUP_HEADS = 4
HEAD_DIM = 64
GROUP_WIDTH = 256
MOBA_BLOCK = 256
MOBA_TOPK = 3
NSA_CMP_LEN = 32
NSA_CMP_STRIDE = 16
NSA_SEL_BLOCK = 64
NSA_SEL_TOPN = 16
NSA_WINDOW = 512
DIFF_QK_DIM = 32
SWA_KV_HEADS = 2
SWA_WINDOW = 128
CONV_WIDTH = 3
NORM_EPS = 1e-6
NEG_INF = -1e30
TINY = 1e-30
FORCED_SCORE = 1e4
LOG2E = 1.4426950408889634
ONES_ROWS = 16
REMOVED = -3e38

TQ = 256
SEL_ROWS = 128
VMEM_LIMIT = 56 * 1024 * 1024

AUG_BLK0 = HEAD_DIM
AUG_POS0 = AUG_BLK0 + SEL_ROWS
AUG_ONE0 = AUG_POS0 + 3


def _alibi_slopes(mixer):
    slopes = np.power(np.float32(2.0), np.arange(1, N_HEADS + 1, dtype=np.float32) * np.float32(-8.0 / N_HEADS))
    return [float(s) for s in slopes[mixer::N_MIXERS]]


def _bf16_pieces(x):
    out, rem = [], np.float32(x)
    for _ in range(3):
        p = np.float32(np.asarray(rem, np.float32).astype(jnp.bfloat16).astype(np.float32))
        out.append(float(p))
        rem = np.float32(rem - p)
    assert rem == 0.0
    return out


def _nt(a, b, precision=None):
    return lax.dot_general(a, b, (((1,), (1,)), ((), ())), preferred_element_type=F32, precision=precision)


def _mm(a, b, precision=None):
    return jnp.dot(a, b, preferred_element_type=F32, precision=precision)


def _rms(x, gain):
    return (x * lax.rsqrt(jnp.mean(x * x, axis=-1, keepdims=True) + NORM_EPS)) * gain


def _params(*sem):
    return pltpu.CompilerParams(dimension_semantics=sem, vmem_limit_bytes=VMEM_LIMIT)


_N_GROUPS = (("mk", 256, BF16), ("nks", 256, BF16), ("nkw", 128, BF16), ("dk", 256, BF16), ("sk", 128, BF16),
             ("ncmp", 128, F32))
_T_GROUPS = (("mq", 256, BF16), ("nq", 256, BF16), ("dq", 256, BF16), ("sq", 256, BF16),
             ("mv", 256, BF16), ("dv", 256, BF16), ("nvs", 64, BF16), ("nvw", 64, BF16), ("sv", 128, BF16),
             ("gates", 768, F32))
_ZERO_COL = 2700


def _in_columns():
    r = np.arange
    z = lambda n: np.full(n, _ZERO_COL)
    ncols = np.concatenate([r(256, 512), r(1152, 1216), z(192), r(1280, 1344), z(64), r(1676, 1932),
                            r(2444, 2572), r(1024, 1152)])
    gate = np.asarray([1408 + 3 * h + j for j in range(3) for h in range(4) for _ in range(64)])
    tcols = np.concatenate([r(0, 256), r(768, 1024), r(1420, 1676), r(2188, 2444), r(512, 768), r(1932, 2188),
                            r(1216, 1280), r(1344, 1408), r(2572, 2700), gate])
    assert ncols.shape[0] == sum(w for _, w, _ in _N_GROUPS) and tcols.shape[0] == sum(w for _, w, _ in _T_GROUPS)
    return ncols, tcols


def _in_proj_kernel(x_ref, g_ref, wn_ref, wt_ref, *out_refs, tm, seq):
    names = [n for n, _, _ in _N_GROUPS] + [n for n, _, _ in _T_GROUPS] + ["kmean"]
    outs = dict(zip(names, out_refs))
    hb = _rms(x_ref[...], g_ref[...]).astype(BF16)
    off = 0
    for name, width, dtype in _N_GROUPS:
        res = _mm(hb, wn_ref[:, off:off + width])
        off += width
        if name == "mk":
            for r in range(tm // MOBA_BLOCK):
                blk = res[r * MOBA_BLOCK:(r + 1) * MOBA_BLOCK]
                outs["kmean"][0, r:r + 1, :] = jnp.mean(blk, axis=0, keepdims=True)
        if name == "nks":
            lane = lax.broadcasted_iota(jnp.int32, res.shape, 1)
            row = lax.broadcasted_iota(jnp.int32, res.shape, 0)
            pos = (pl.program_id(0) % (seq // tm)) * tm + row
            onehot = (lane - AUG_BLK0) == lax.shift_right_logical(pos, 6)
            feat = jnp.where((lane >= AUG_POS0) & (lane < AUG_ONE0), (row % TQ).astype(F32),
                             jnp.where((lane >= AUG_ONE0) & (lane < AUG_ONE0 + 3), 1.0, 0.0))
            res = jnp.where(lane < AUG_BLK0, res, jnp.where(lane < AUG_POS0, jnp.where(onehot, 1.0, 0.0), feat))
        outs[name][...] = res.astype(dtype)
    off = 0
    for name, width, dtype in _T_GROUPS:
        res = _nt(wt_ref[off:off + width, :], hb)
        off += width
        if name == "gates":
            res = 1.0 / (1.0 + jnp.exp(-res))
        for c in range(tm // TQ):
            outs[name][c] = res[:, c * TQ:(c + 1) * TQ].astype(dtype)


def _in_proj(x2, gain, wn, wt, seq):
    n, d = x2.shape
    tm = 512
    out_shape = [jax.ShapeDtypeStruct((n, wd), dt) for _, wd, dt in _N_GROUPS]
    out_specs = [pl.BlockSpec((tm, wd), lambda i: (i, 0)) for _, wd, _ in _N_GROUPS]
    out_shape += [jax.ShapeDtypeStruct((n // TQ, wd, TQ), dt) for _, wd, dt in _T_GROUPS]
    out_specs += [pl.BlockSpec((tm // TQ, wd, TQ), lambda i: (i, 0, 0)) for _, wd, _ in _T_GROUPS]
    out_shape.append(jax.ShapeDtypeStruct((n // tm, tm // MOBA_BLOCK, 256), F32))
    out_specs.append(pl.BlockSpec((1, tm // MOBA_BLOCK, 256), lambda i: (i, 0, 0)))
    outs = pl.pallas_call(
        functools.partial(_in_proj_kernel, tm=tm, seq=seq),
        grid=(n // tm,),
        in_specs=[pl.BlockSpec((tm, d), lambda i: (i, 0)),
                  pl.BlockSpec((1, d), lambda i: (0, 0)),
                  pl.BlockSpec(wn.shape, lambda i: (0, 0)),
                  pl.BlockSpec(wt.shape, lambda i: (0, 0))],
        out_specs=out_specs,
        out_shape=out_shape,
        compiler_params=_params("parallel"),
        name="in_proj",
    )(x2, gain.reshape(1, d), wn, wt)
    names = [nm for nm, _, _ in _N_GROUPS] + [nm for nm, _, _ in _T_GROUPS] + ["kmean"]
    return dict(zip(names, outs))


def _rel_pos(tq):
    key = lax.broadcasted_iota(jnp.int32, (tq, tq), 0)
    qry = lax.broadcasted_iota(jnp.int32, (tq, tq), 1)
    return key, qry


def _head_rows(h, n=1):
    return slice(h * HEAD_DIM, (h + n) * HEAD_DIM)


def _flash_init(m_ref, acc_ref):
    m_ref[...] = jnp.full(m_ref.shape, NEG_INF, F32)
    acc_ref[...] = jnp.zeros(acc_ref.shape, F32)


def _with_ones(vt):
    return jnp.concatenate([vt, jnp.ones((ONES_ROWS, vt.shape[1]), BF16)], axis=0)


def _score(s_ref, c, sv):
    s_ref[c] = sv
    tk, tq = sv.shape
    return jnp.max(sv.reshape(tk // 8, 8, tq), axis=0)


def _flash_update(s_ref, c, mx8, cj, vt, idx, m_ref, acc_ref):
    tk = s_ref.shape[1]
    m_old = m_ref[idx:idx + 1, :]
    m_new = jnp.maximum(m_old, jnp.max(mx8, axis=0, keepdims=True) + cj)
    shift = m_new - cj
    ps = [jnp.exp2(s_ref[c, r * 64:(r + 1) * 64, :] - shift).astype(BF16) for r in range(tk // 64)]
    alpha = jnp.exp2(m_old - m_new)
    acc_ref[idx] = alpha * acc_ref[idx] + _mm(vt, jnp.concatenate(ps, axis=0))
    m_ref[idx:idx + 1, :] = m_new


def _normalized(acc_ref, idx):
    a = acc_ref[idx]
    return a[0:HEAD_DIM] / a[HEAD_DIM:HEAD_DIM + 1]


def _topk_bias_t(score_t, k):
    nrow = score_t.shape[0]
    rows = lax.broadcasted_iota(jnp.int32, score_t.shape, 0).astype(F32)
    bias = jnp.full(score_t.shape, NEG_INF, F32)
    work = score_t
    for _ in range(k):
        m = jnp.max(work, axis=0, keepdims=True)
        idx = jnp.min(jnp.where(work == m, rows, float(nrow)), axis=0, keepdims=True)
        pick = rows == idx
        bias = jnp.where(pick, jnp.where(m > 0.5 * NEG_INF, 0.0, NEG_INF), bias)
        work = jnp.where(pick, REMOVED, work)
    return bias


def _k_tile(ref, j, tk):
    return ref[0, pl.ds(pl.multiple_of(j * tk, tk), tk), :]


def _qt_spec(width, nq):
    return pl.BlockSpec((1, width, TQ), lambda b, i: (b * nq + i, 0, 0))


def _k_spec(s, width):
    return pl.BlockSpec((1, s, width), lambda b, i: (b, 0, 0))


def _vt_spec(nq, width):
    return pl.BlockSpec((nq, width, TQ), lambda b, i: (b, 0, 0))


def _row_spec(width, nq):
    return pl.BlockSpec((TQ, width), lambda b, i: (b * nq + i, 0))


def _flash_scratch(chains, tiles):
    return [pltpu.VMEM((8, TQ), F32), pltpu.VMEM((chains, HEAD_DIM + ONES_ROWS, TQ), F32),
            pltpu.VMEM((tiles, TQ, TQ), F32), pltpu.VMEM((GROUP_WIDTH, TQ), F32)]


def _moba_kernel(qt_ref, k_ref, vt_ref, km_ref, o_ref, rhs_ref, tab_ref, sel_ref, m_ref, acc_ref, s_ref, ot_ref,
                 *, slopes, tq):
    i = pl.program_id(1)
    nh = GROUP_HEADS
    scale = HEAD_DIM ** -0.5 * LOG2E
    qt = qt_ref[0].astype(F32)
    key, qry = _rel_pos(tq)
    rel = (qry - key).astype(F32)
    causal = key <= qry
    row_head = lax.broadcasted_iota(jnp.int32, (GROUP_WIDTH, 1), 0) // HEAD_DIM
    blk_rows = lax.broadcasted_iota(jnp.int32, (SEL_ROWS, tq), 0)
    _flash_init(m_ref, acc_ref)
    kd = _k_tile(k_ref, i, tq)
    for h in range(nh):
        qh = jnp.where(row_head == h, qt, 0.0)
        gate_t = _mm(km_ref[0], qh, precision=HIGHEST)
        gate_t = jnp.where(blk_rows < i, gate_t, NEG_INF)
        sel_ref[h] = _topk_bias_t(gate_t, MOBA_TOPK)
        rhs_ref[h] = (qh * scale).astype(BF16)
        tab_ref[h] = rel * (-slopes[h] * LOG2E)
    mx = [_score(s_ref, h, jnp.where(causal, _mm(kd, rhs_ref[h]) + tab_ref[h], NEG_INF)) for h in range(nh)]
    for h in range(nh):
        _flash_update(s_ref, h, mx[h], 0.0, _with_ones(vt_ref[i, _head_rows(h), :]), h, m_ref, acc_ref)

    def body(j, carry):
        kj = _k_tile(k_ref, j, tq)
        dj = (i - j).astype(F32)
        mx = [_score(s_ref, h, _mm(kj, rhs_ref[h]) + tab_ref[h]) for h in range(nh)]
        for h in range(nh):
            cj = dj * (-slopes[h] * LOG2E * tq) + sel_ref[h, pl.ds(j, 1), :]
            _flash_update(s_ref, h, mx[h], cj, _with_ones(vt_ref[j, _head_rows(h), :]), h, m_ref, acc_ref)
        return carry

    lax.fori_loop(0, i, body, 0)
    for h in range(nh):
        ot_ref[_head_rows(h), :] = _normalized(acc_ref, h)
    o_ref[...] = ot_ref[...].T.astype(o_ref.dtype)


def _moba(qt, k, vt, kmean, b, s):
    nq = s // TQ
    return pl.pallas_call(
        functools.partial(_moba_kernel, slopes=_alibi_slopes(0), tq=TQ),
        grid=(b, nq),
        in_specs=[_qt_spec(GROUP_WIDTH, nq), _k_spec(s, GROUP_WIDTH), _vt_spec(nq, GROUP_WIDTH),
                  pl.BlockSpec((1, SEL_ROWS, GROUP_WIDTH), lambda bb, i: (bb, 0, 0))],
        out_specs=_row_spec(GROUP_WIDTH, nq),
        out_shape=jax.ShapeDtypeStruct((b * s, GROUP_WIDTH), BF16),
        scratch_shapes=[pltpu.VMEM((GROUP_HEADS, GROUP_WIDTH, TQ), BF16),
                        pltpu.VMEM((GROUP_HEADS, TQ, TQ), F32),
                        pltpu.VMEM((GROUP_HEADS, SEL_ROWS, TQ), F32)] + _flash_scratch(GROUP_HEADS, GROUP_HEADS),
        compiler_params=_params("parallel", "arbitrary"),
        name="moba",
    )(qt, k.reshape(b, s, GROUP_WIDTH), vt, kmean)


def _nsa_cmp_kernel(x_ref, pa_ref, pb_ref, w1a_ref, w1b_ref, b1_ref, w2_ref, o_ref, *, transposed):
    x = x_ref[0]
    n = x.shape[0]
    a = _mm(x + pa_ref[...], w1a_ref[...], precision=HIGHEST)
    bm = _mm(x + pb_ref[...], w1b_ref[...], precision=HIGHEST)
    hid = jax.nn.gelu(a + pltpu.roll(bm, n - 1, 0) + b1_ref[...], approximate=True)
    if transposed:
        o_ref[0] = _nt(w2_ref[...], hid, precision=HIGHEST)
    else:
        o_ref[0] = _mm(hid, w2_ref[...], precision=HIGHEST)


def _nsa_compress(x, pos, w1, b1, w2, b, s, transposed):
    rows = s // NSA_CMP_STRIDE
    half = NSA_CMP_STRIDE * HEAD_DIM
    hid = w1.shape[-1]
    full = lambda shape: pl.BlockSpec(shape, lambda bb: (0,) * len(shape))
    if transposed:
        w2 = w2.T
        oshape = (b, HEAD_DIM, rows)
    else:
        w2 = jnp.pad(w2, ((0, 0), (0, 128 - HEAD_DIM)))
        oshape = (b, rows, 128)
    return pl.pallas_call(
        functools.partial(_nsa_cmp_kernel, transposed=transposed),
        grid=(b,),
        in_specs=[pl.BlockSpec((1, rows, half), lambda bb: (bb, 0, 0)),
                  full((1, half)), full((1, half)), full((half, hid)), full((half, hid)),
                  full((1, hid)), full(w2.shape)],
        out_specs=pl.BlockSpec((1,) + oshape[1:], lambda bb: (bb, 0, 0)),
        out_shape=jax.ShapeDtypeStruct(oshape, F32),
        compiler_params=_params("parallel"),
        name="nsa_compress",
    )(x.reshape(b, rows, half),
      pos[:NSA_CMP_STRIDE].reshape(1, half), pos[NSA_CMP_STRIDE:].reshape(1, half),
      w1[:NSA_CMP_STRIDE].reshape(half, hid), w1[NSA_CMP_STRIDE:].reshape(half, hid),
      b1.reshape(1, hid), w2)


def _nsa_sel_kernel(qt_ref, kc_ref, vct_ref, ov_ref, oc_ref, sel_ref, *, slopes, tq):
    i = pl.program_id(1)
    scale = HEAD_DIM ** -0.5
    nc = kc_ref.shape[1]
    qt = qt_ref[0].astype(F32)
    kc = kc_ref[0].astype(BF16)
    vct = vct_ref[0].astype(BF16)
    t = i * tq + lax.broadcasted_iota(jnp.int32, (nc, tq), 1)
    cmp_end = lax.broadcasted_iota(jnp.int32, (nc, tq), 0) * NSA_CMP_STRIDE + (NSA_CMP_LEN - 1)
    dist = (t - cmp_end).astype(F32)
    mask = cmp_end <= t
    zeros = jnp.zeros((HEAD_DIM, tq), BF16)
    p_sum = jnp.zeros((nc, tq), F32)
    for h in range(GROUP_HEADS):
        qs = (qt[_head_rows(h), :] * scale).astype(BF16)
        sc = _mm(kc, jnp.concatenate([qs, zeros], axis=0)) - slopes[h] * dist
        sc = jnp.where(mask, sc, NEG_INF)
        m = jnp.max(sc, axis=0, keepdims=True)
        p = jnp.where(mask, jnp.exp(sc - m), 0.0)
        p = p / jnp.maximum(jnp.sum(p, axis=0, keepdims=True), TINY)
        oc_ref[0, _head_rows(h), :] = _mm(vct, p.astype(BF16))
        p_sum = p_sum + p
    imp_t = _mm(ov_ref[...], p_sum, precision=HIGHEST)
    blk = lax.broadcasted_iota(jnp.int32, (SEL_ROWS, tq), 0)
    tt = i * tq + lax.broadcasted_iota(jnp.int32, (SEL_ROWS, tq), 1)
    qblk = lax.shift_right_logical(tt, 6)
    forced = (blk == 0) | (blk == qblk) | (blk == qblk - 1)
    score = jnp.where(forced, FORCED_SCORE, imp_t)
    score = jnp.where(blk * NSA_SEL_BLOCK <= tt, score, NEG_INF)
    sel_ref[0] = _topk_bias_t(score, NSA_SEL_TOPN).astype(sel_ref.dtype)


def _nsa_select(qt, kc, vct, b, s):
    nq = s // TQ
    nc = s // NSA_CMP_STRIDE
    nsel = s // NSA_SEL_BLOCK
    assert nsel <= SEL_ROWS and NSA_SEL_TOPN <= nsel
    cs = np.arange(nc)[None, :] * NSA_CMP_STRIDE
    bs = np.arange(SEL_ROWS)[:, None] * NSA_SEL_BLOCK
    ov = np.clip(np.minimum(cs + NSA_CMP_LEN, bs + NSA_SEL_BLOCK) - np.maximum(cs, bs), 0, None) / NSA_CMP_LEN
    ov[:, nc - 1] = 0.0
    ov[nsel:] = 0.0
    return pl.pallas_call(
        functools.partial(_nsa_sel_kernel, slopes=_alibi_slopes(1), tq=TQ),
        grid=(b, nq),
        in_specs=[_qt_spec(GROUP_WIDTH, nq),
                  pl.BlockSpec((1, nc, 128), lambda bb, i: (bb, 0, 0)),
                  pl.BlockSpec((1, HEAD_DIM, nc), lambda bb, i: (bb, 0, 0)),
                  pl.BlockSpec((SEL_ROWS, nc), lambda bb, i: (0, 0))],
        out_specs=[_qt_spec(GROUP_WIDTH, nq), _qt_spec(SEL_ROWS, nq)],
        out_shape=[jax.ShapeDtypeStruct((b * nq, GROUP_WIDTH, TQ), F32),
                   jax.ShapeDtypeStruct((b * nq, SEL_ROWS, TQ), BF16)],
        compiler_params=_params("parallel", "arbitrary"),
        name="nsa_select",
    )(qt, kc, vct, jnp.asarray(ov.astype(np.float32)))


def _nsa_attn_kernel(qt_ref, ks_ref, vst_ref, kw_ref, vwt_ref, sel_ref, oc_ref, g_ref, o_ref,
                     rhs_ref, rhsw_ref, tab_ref, m_ref, acc_ref, s_ref, ot_ref, *, slopes, tq):
    i = pl.program_id(1)
    scale = HEAD_DIM ** -0.5 * LOG2E
    nh = GROUP_HEADS
    qt = qt_ref[0].astype(F32)
    key, qry = _rel_pos(tq)
    rel = (qry - key).astype(F32)
    causal = key <= qry
    frow = lax.broadcasted_iota(jnp.int32, (HEAD_DIM, tq), 0)
    qoff = lax.broadcasted_iota(jnp.int32, (1, tq), 1).astype(F32)
    zeros = jnp.zeros((HEAD_DIM, tq), BF16)
    _flash_init(m_ref, acc_ref)
    ksd = _k_tile(ks_ref, i, tq)
    kwd = _k_tile(kw_ref, i, tq)
    for h in range(nh):
        qs = (qt[_head_rows(h), :] * scale).astype(BF16)
        slope2 = float(np.float32(slopes[h] * LOG2E))
        s1, s2, s3 = _bf16_pieces(slope2)
        t = qoff * (-slope2)
        t1 = t.astype(BF16).astype(F32)
        t2 = (t - t1).astype(BF16).astype(F32)
        t3 = t - t1 - t2
        feat = jnp.where(frow == 0, s1, jnp.where(frow == 1, s2, jnp.where(frow == 2, s3, jnp.where(
            frow == 3, t1, jnp.where(frow == 4, t2, jnp.where(frow == 5, t3, 0.0))))))
        rhs_ref[h, 0:AUG_BLK0, :] = qs
        rhs_ref[h, AUG_BLK0:AUG_POS0, :] = sel_ref[0]
        rhs_ref[h, AUG_POS0:GROUP_WIDTH, :] = feat.astype(BF16)
        rhsw_ref[h] = jnp.concatenate([qs, zeros], axis=0)
        tab_ref[h] = rel * (-slope2)
    vsd = _with_ones(vst_ref[i])
    vwd = _with_ones(vwt_ref[i])
    mx = [_score(s_ref, h, jnp.where(causal, _mm(ksd, rhs_ref[h]), NEG_INF)) for h in range(nh)]
    for h in range(nh):
        _flash_update(s_ref, h, mx[h], 0.0, vsd, h, m_ref, acc_ref)
    mx = [_score(s_ref, h, jnp.where(causal, _mm(kwd, rhsw_ref[h]) + tab_ref[h], NEG_INF)) for h in range(nh)]
    for h in range(nh):
        _flash_update(s_ref, h, mx[h], 0.0, vwd, nh + h, m_ref, acc_ref)

    def body(j, carry):
        ksj = _k_tile(ks_ref, j, tq)
        vsj = _with_ones(vst_ref[j])
        dj = (i - j).astype(F32)
        mx = [_score(s_ref, h, _mm(ksj, rhs_ref[h])) for h in range(nh)]
        for h in range(nh):
            _flash_update(s_ref, h, mx[h], dj * (-slopes[h] * LOG2E * tq), vsj, h, m_ref, acc_ref)
        return carry

    lax.fori_loop(0, i, body, 0)

    for back in range(1, NSA_WINDOW // tq + 1):
        @pl.when(i >= back)
        def _(back=back):
            kwj = _k_tile(kw_ref, i - back, tq)
            vwj = _with_ones(vwt_ref[i - back])
            mx = []
            for h in range(nh):
                sv = _mm(kwj, rhsw_ref[h]) + tab_ref[h]
                if back * tq == NSA_WINDOW:
                    sv = jnp.where(key > qry, sv, NEG_INF)
                mx.append(_score(s_ref, h, sv))
            for h in range(nh):
                _flash_update(s_ref, h, mx[h], -slopes[h] * LOG2E * tq * back, vwj, nh + h, m_ref, acc_ref)

    for h in range(nh):
        rows = _head_rows(h)
        ot_ref[rows, :] = (g_ref[0, rows, :] * oc_ref[0, rows, :]
                           + g_ref[0, _head_rows(nh + h), :] * _normalized(acc_ref, h)
                           + g_ref[0, _head_rows(2 * nh + h), :] * _normalized(acc_ref, nh + h))
    o_ref[...] = ot_ref[...].T.astype(o_ref.dtype)


def _nsa_attend(qt, ks, vst, kw, vwt, sel, oct, gt, b, s):
    nq = s // TQ
    return pl.pallas_call(
        functools.partial(_nsa_attn_kernel, slopes=_alibi_slopes(1), tq=TQ),
        grid=(b, nq),
        in_specs=[_qt_spec(GROUP_WIDTH, nq), _k_spec(s, GROUP_WIDTH), _vt_spec(nq, HEAD_DIM),
                  _k_spec(s, 128), _vt_spec(nq, HEAD_DIM), _qt_spec(SEL_ROWS, nq), _qt_spec(GROUP_WIDTH, nq),
                  _qt_spec(3 * GROUP_WIDTH, nq)],
        out_specs=_row_spec(GROUP_WIDTH, nq),
        out_shape=jax.ShapeDtypeStruct((b * s, GROUP_WIDTH), BF16),
        scratch_shapes=[pltpu.VMEM((GROUP_HEADS, GROUP_WIDTH, TQ), BF16),
                        pltpu.VMEM((GROUP_HEADS, 128, TQ), BF16),
                        pltpu.VMEM((GROUP_HEADS, TQ, TQ), F32)] + _flash_scratch(2 * GROUP_HEADS, GROUP_HEADS),
        compiler_params=_params("parallel", "arbitrary"),
        name="nsa_attend",
    )(qt, ks.reshape(b, s, GROUP_WIDTH), vst, kw.reshape(b, s, 128), vwt, sel, oct, gt)


def _diff_kernel(qt_ref, k_ref, vt_ref, lam_ref, sub_ref, o_ref, rhs_ref, tab_ref, m_ref, acc_ref, s_ref, ot_ref,
                 *, slopes, tq, lambda_init):
    i = pl.program_id(1)
    scale = DIFF_QK_DIM ** -0.5 * LOG2E
    nh = GROUP_HEADS
    qt = qt_ref[0].astype(F32)
    key, qry = _rel_pos(tq)
    rel = (qry - key).astype(F32)
    causal = key <= qry
    row = lax.broadcasted_iota(jnp.int32, (GROUP_WIDTH, 1), 0)
    _flash_init(m_ref, acc_ref)
    kd = _k_tile(k_ref, i, tq)
    for h in range(nh):
        lo = h * HEAD_DIM
        rhs_ref[h] = (jnp.where((row >= lo) & (row < lo + DIFF_QK_DIM), qt, 0.0) * scale).astype(BF16)
        rhs_ref[nh + h] = (jnp.where((row >= lo + DIFF_QK_DIM) & (row < lo + HEAD_DIM), qt, 0.0)
                           * scale).astype(BF16)
        tab_ref[h] = rel * (-slopes[h] * LOG2E)
    mx = [_score(s_ref, c, jnp.where(causal, _mm(kd, rhs_ref[c]) + tab_ref[c % nh], NEG_INF))
          for c in range(2 * nh)]
    for c in range(2 * nh):
        _flash_update(s_ref, c, mx[c], 0.0, _with_ones(vt_ref[i, _head_rows(c % nh), :]), c, m_ref, acc_ref)

    def body(j, carry):
        kj = _k_tile(k_ref, j, tq)
        dj = (i - j).astype(F32)
        mx = [_score(s_ref, c, _mm(kj, rhs_ref[c]) + tab_ref[c % nh]) for c in range(2 * nh)]
        for c in range(2 * nh):
            h = c % nh
            _flash_update(s_ref, c, mx[c], dj * (-slopes[h] * LOG2E * tq),
                          _with_ones(vt_ref[j, _head_rows(h), :]), c, m_ref, acc_ref)
        return carry

    lax.fori_loop(0, i, body, 0)
    lam_p = lam_ref[...]
    lam = (jnp.exp(jnp.sum(lam_p[0:1] * lam_p[1:2], axis=-1, keepdims=True))
           - jnp.exp(jnp.sum(lam_p[2:3] * lam_p[3:4], axis=-1, keepdims=True)) + lambda_init)
    for h in range(nh):
        o = _normalized(acc_ref, h) - lam * _normalized(acc_ref, nh + h)
        ms = jnp.mean(o * o, axis=0, keepdims=True)
        ot_ref[_head_rows(h), :] = ((o * lax.rsqrt(ms + NORM_EPS)) * sub_ref[...]) * (1.0 - lambda_init)
    o_ref[...] = ot_ref[...].T.astype(o_ref.dtype)


def _diff(qt, k, vt, lam, subln, lambda_init, b, s):
    nq = s // TQ
    return pl.pallas_call(
        functools.partial(_diff_kernel, slopes=_alibi_slopes(2), tq=TQ, lambda_init=lambda_init),
        grid=(b, nq),
        in_specs=[_qt_spec(GROUP_WIDTH, nq), _k_spec(s, GROUP_WIDTH), _vt_spec(nq, GROUP_WIDTH),
                  pl.BlockSpec((4, DIFF_QK_DIM), lambda bb, i: (0, 0)),
                  pl.BlockSpec((HEAD_DIM, TQ), lambda bb, i: (0, 0))],
        out_specs=_row_spec(GROUP_WIDTH, nq),
        out_shape=jax.ShapeDtypeStruct((b * s, GROUP_WIDTH), BF16),
        scratch_shapes=[pltpu.VMEM((2 * GROUP_HEADS, GROUP_WIDTH, TQ), BF16),
                        pltpu.VMEM((GROUP_HEADS, TQ, TQ), F32)] + _flash_scratch(2 * GROUP_HEADS, 2 * GROUP_HEADS),
        compiler_params=_params("parallel", "arbitrary"),
        name="diff_attn",
    )(qt, k.reshape(b, s, GROUP_WIDTH), vt, lam, jnp.broadcast_to(subln[:, None], (HEAD_DIM, TQ)))


def _swa_kernel(qt_ref, k_ref, vt_ref, sink_ref, o_ref, rhs_ref, tab_ref, m_ref, acc_ref, s_ref, ot_ref,
                *, slopes, tq):
    i = pl.program_id(1)
    scale = HEAD_DIM ** -0.5 * LOG2E
    nh = GROUP_HEADS
    per_kv = nh // SWA_KV_HEADS
    qt = qt_ref[0].astype(F32)
    key, qry = _rel_pos(tq)
    d = qry - key
    rel = d.astype(F32)
    zeros = jnp.zeros((HEAD_DIM, tq), BF16)
    _flash_init(m_ref, acc_ref)
    kd = _k_tile(k_ref, i, tq)
    for h in range(nh):
        qs = (qt[_head_rows(h), :] * scale).astype(BF16)
        rhs_ref[h] = jnp.concatenate([qs, zeros] if h // per_kv == 0 else [zeros, qs], axis=0)
        tab_ref[h] = rel * (-slopes[h] * LOG2E)
    in_window = (d >= 0) & (d < SWA_WINDOW)
    mx = [_score(s_ref, h, jnp.where(in_window, _mm(kd, rhs_ref[h]) + tab_ref[h], NEG_INF)) for h in range(nh)]
    for h in range(nh):
        _flash_update(s_ref, h, mx[h], 0.0, _with_ones(vt_ref[i, _head_rows(h // per_kv), :]), h, m_ref, acc_ref)

    @pl.when(i >= 1)
    def _():
        kj = _k_tile(k_ref, i - 1, tq)
        mx = [_score(s_ref, h, jnp.where(d + tq < SWA_WINDOW, _mm(kj, rhs_ref[h]) + tab_ref[h], NEG_INF))
              for h in range(nh)]
        for h in range(nh):
            _flash_update(s_ref, h, mx[h], -slopes[h] * LOG2E * tq,
                          _with_ones(vt_ref[i - 1, _head_rows(h // per_kv), :]), h, m_ref, acc_ref)

    for h in range(nh):
        sink = sink_ref[h:h + 1, 0:1] * LOG2E
        m_old = m_ref[h:h + 1, :]
        m_new = jnp.maximum(m_old, sink)
        alpha = jnp.exp2(m_old - m_new)
        a = acc_ref[h]
        denom = alpha * a[HEAD_DIM:HEAD_DIM + 1] + jnp.exp2(sink - m_new)
        ot_ref[_head_rows(h), :] = (alpha * a[0:HEAD_DIM]) / denom
    o_ref[...] = ot_ref[...].T.astype(o_ref.dtype)


def _swa(qt, k, vt, sinks, b, s):
    nq = s // TQ
    kvw = SWA_KV_HEADS * HEAD_DIM
    return pl.pallas_call(
        functools.partial(_swa_kernel, slopes=_alibi_slopes(3), tq=TQ),
        grid=(b, nq),
        in_specs=[_qt_spec(GROUP_WIDTH, nq), _k_spec(s, kvw), _vt_spec(nq, kvw),
                  pl.BlockSpec((8, 128), lambda bb, i: (0, 0))],
        out_specs=_row_spec(GROUP_WIDTH, nq),
        out_shape=jax.ShapeDtypeStruct((b * s, GROUP_WIDTH), BF16),
        scratch_shapes=[pltpu.VMEM((GROUP_HEADS, kvw, TQ), BF16),
                        pltpu.VMEM((GROUP_HEADS, TQ, TQ), F32)] + _flash_scratch(GROUP_HEADS, GROUP_HEADS),
        compiler_params=_params("parallel", "arbitrary"),
        name="swa",
    )(qt, k.reshape(b, s, kvw), vt, jnp.pad(jnp.broadcast_to(sinks[:, None], (GROUP_HEADS, 128)), ((0, 4), (0, 0))))


def _out_proj_kernel(x_ref, a_ref, b_ref, c_ref, d_ref, w_ref, g_ref, o_ref):
    y = None
    for g, m_ref in enumerate((a_ref, b_ref, c_ref, d_ref)):
        part = _mm(m_ref[...], w_ref[g * GROUP_WIDTH:(g + 1) * GROUP_WIDTH, :])
        y = part if y is None else y + part
    o_ref[...] = x_ref[...] + _rms(y, g_ref[...])


def _out_proj(x2, mixes, w, gain):
    n, d = x2.shape
    tm = 512
    row = lambda wd: pl.BlockSpec((tm, wd), lambda i: (i, 0))
    return pl.pallas_call(
        _out_proj_kernel,
        grid=(n // tm,),
        in_specs=[row(d)] + [row(GROUP_WIDTH)] * 4 + [pl.BlockSpec((d, d), lambda i: (0, 0)),
                                                     pl.BlockSpec((1, d), lambda i: (0, 0))],
        out_specs=row(d),
        out_shape=jax.ShapeDtypeStruct((n, d), F32),
        compiler_params=_params("parallel"),
        name="out_proj",
    )(x2, *mixes, w, gain.reshape(1, d))


def _ffn_kernel(x_ref, xp_ref, gpre_ref, wg_ref, wu_ref, cw_ref, cb_ref, wd_ref, gpost_ref, o_ref,
                h_ref, hp_ref, acc_ref, *, tiles_per_seq):
    i = pl.program_id(0)
    f = pl.program_id(1)

    @pl.when(f == 0)
    def _():
        h_ref[...] = _rms(x_ref[...], gpre_ref[...]).astype(BF16)
        hp = _rms(xp_ref[...], gpre_ref[...])
        hp_ref[...] = jnp.where(i % tiles_per_seq != 0, hp, 0.0).astype(BF16)
        acc_ref[...] = jnp.zeros(acc_ref.shape, F32)

    h = h_ref[...]
    a = _mm(h, wg_ref[...])
    ap = _mm(hp_ref[...], wg_ref[...])
    row = lax.broadcasted_iota(jnp.int32, a.shape, 0)
    a1 = jnp.where(row == 0, ap[7:8], pltpu.roll(a, 1, 0))
    a2 = jnp.where(row == 0, ap[6:7], jnp.where(row == 1, ap[7:8], pltpu.roll(a, 2, 0)))
    cw = cw_ref[...]
    conv = cw[0:1] * a2 + cw[1:2] * a1 + cw[2:3] * a + cb_ref[...]
    gated = jax.nn.gelu(conv, approximate=True) * _mm(h, wu_ref[...])
    acc_ref[...] += _mm(gated.astype(BF16), wd_ref[...])

    @pl.when(f == pl.num_programs(1) - 1)
    def _():
        o_ref[...] = x_ref[...] + _rms(acc_ref[...], gpost_ref[...])


def _ffn(x2, gpre, wg, wu, cw, cb, wd, gpost, s):
    n, d = x2.shape
    dff = wg.shape[1]
    tm, tf = 1024, 512
    halo = 8
    return pl.pallas_call(
        functools.partial(_ffn_kernel, tiles_per_seq=s // tm),
        grid=(n // tm, dff // tf),
        in_specs=[pl.BlockSpec((tm, d), lambda i, f: (i, 0)),
                  pl.BlockSpec((halo, d), lambda i, f: (jnp.maximum(i * (tm // halo) - 1, 0), 0)),
                  pl.BlockSpec((1, d), lambda i, f: (0, 0)),
                  pl.BlockSpec((d, tf), lambda i, f: (0, f)),
                  pl.BlockSpec((d, tf), lambda i, f: (0, f)),
                  pl.BlockSpec((CONV_WIDTH, tf), lambda i, f: (0, f)),
                  pl.BlockSpec((1, tf), lambda i, f: (0, f)),
                  pl.BlockSpec((tf, d), lambda i, f: (f, 0)),
                  pl.BlockSpec((1, d), lambda i, f: (0, 0))],
        out_specs=pl.BlockSpec((tm, d), lambda i, f: (i, 0)),
        out_shape=jax.ShapeDtypeStruct((n, d), F32),
        scratch_shapes=[pltpu.VMEM((tm, d), BF16), pltpu.VMEM((halo, d), BF16), pltpu.VMEM((tm, d), F32)],
        compiler_params=_params("parallel", "arbitrary"),
        name="conv_ffn",
    )(x2, x2, gpre.reshape(1, d), wg, wu, cw, cb.reshape(1, dff), wd, gpost.reshape(1, d))


def kernel(x, attn_pre_norm, attn_post_norm, ffn_pre_norm, ffn_post_norm, w_in, w_out, nsa_cmp_pos_k, nsa_cmp_w1_k, nsa_cmp_b1_k, nsa_cmp_w2_k, nsa_cmp_pos_v, nsa_cmp_w1_v, nsa_cmp_b1_v, nsa_cmp_w2_v, diff_lambda_q1, diff_lambda_k1, diff_lambda_q2, diff_lambda_k2, diff_subln, swa_sinks, ffn_w_gate, ffn_w_up, ffn_conv_w, ffn_conv_b, ffn_w_down):
    b, s, d = x.shape
    depth = w_in.shape[0]
    assert s % 1024 == 0 and s // MOBA_BLOCK <= SEL_ROWS
    ncols, tcols = (jnp.asarray(c) for c in _in_columns())
    x2 = x.reshape(b * s, d)
    for l in range(depth):
        w_ext = jnp.pad(w_in[l], ((0, 0), (0, 1)))
        p = _in_proj(x2, attn_pre_norm[l], jnp.take(w_ext, ncols, axis=1).astype(BF16),
                     jnp.take(w_ext, tcols, axis=1).T.astype(BF16), s)
        kmean = p["kmean"].reshape(b, s // MOBA_BLOCK, GROUP_WIDTH)
        kmean = jnp.pad(kmean, ((0, 0), (0, SEL_ROWS - s // MOBA_BLOCK), (0, 0)))
        o_moba = _moba(p["mq"], p["mk"], p["mv"], kmean, b, s)
        kc = _nsa_compress(p["ncmp"][:, :HEAD_DIM], nsa_cmp_pos_k[l], nsa_cmp_w1_k[l], nsa_cmp_b1_k[l],
                           nsa_cmp_w2_k[l], b, s, False)
        vct = _nsa_compress(p["ncmp"][:, HEAD_DIM:], nsa_cmp_pos_v[l], nsa_cmp_w1_v[l], nsa_cmp_b1_v[l],
                            nsa_cmp_w2_v[l], b, s, True)
        o_cmp, sel = _nsa_select(p["nq"], kc, vct, b, s)
        o_nsa = _nsa_attend(p["nq"], p["nks"], p["nvs"], p["nkw"], p["nvw"], sel, o_cmp, p["gates"], b, s)
        lam = jnp.stack([diff_lambda_q1[l], diff_lambda_k1[l], diff_lambda_q2[l], diff_lambda_k2[l]])
        lambda_init = 0.8 - 0.6 * math.exp(-0.3 * l)
        o_diff = _diff(p["dq"], p["dk"], p["dv"], lam, diff_subln[l], lambda_init, b, s)
        o_swa = _swa(p["sq"], p["sk"], p["sv"], swa_sinks[l], b, s)
        x2 = _out_proj(x2, (o_moba, o_nsa, o_diff, o_swa), w_out[l].astype(BF16), attn_post_norm[l])
        x2 = _ffn(x2, ffn_pre_norm[l], ffn_w_gate[l].astype(BF16), ffn_w_up[l].astype(BF16), ffn_conv_w[l],
                  ffn_conv_b[l], ffn_w_down[l].astype(BF16), ffn_post_norm[l], s)
    return x2.reshape(b, s, d)
```

```python
import functools
import math

import numpy as np
import jax
import jax.numpy as jnp
from jax import lax
from jax.experimental import pallas as pl
from jax.experimental.pallas import tpu as pltpu

F32 = jnp.float32
BF16 = jnp.bfloat16
HIGHEST = lax.Precision.HIGHEST

N_MIXERS = 4
N_HEADS = 16
GROUP_HEADS = 4
HEAD_DIM = 64
GROUP_WIDTH = 256
MOBA_BLOCK = 256
MOBA_TOPK = 3
NSA_CMP_LEN = 32
NSA_CMP_STRIDE = 16
NSA_SEL_BLOCK = 64
NSA_SEL_TOPN = 16
NSA_WINDOW = 512
DIFF_QK_DIM = 32
SWA_KV_HEADS = 2
SWA_WINDOW = 128
CONV_WIDTH = 3
NORM_EPS = 1e-6
NEG_INF = -1e30
TINY = 1e-30
FORCED_SCORE = 1e4
LOG2E = 1.4426950408889634
ONES_ROWS = 16
UNDERFLOW_MARGIN = 160.0
REMOVED = -3e38

TQ = 256
SWEEP_GROUP = 4
SEL_ROWS = 128
VMEM_LIMIT = 56 * 1024 * 1024

AUG_BLK0 = HEAD_DIM
AUG_POS0 = AUG_BLK0 + SEL_ROWS
AUG_ONE0 = AUG_POS0 + 3


def _alibi_slopes(mixer):
    slopes = np.power(np.float32(2.0), np.arange(1, N_HEADS + 1, dtype=np.float32) * np.float32(-8.0 / N_HEADS))
    return [float(s) for s in slopes[mixer::N_MIXERS]]


def _bf16_pieces(x):
    out, rem = [], np.float32(x)
    for _ in range(3):
        p = np.float32(np.asarray(rem, np.float32).astype(jnp.bfloat16).astype(np.float32))
        out.append(float(p))
        rem = np.float32(rem - p)
    assert rem == 0.0
    return out


def _nt(a, b, precision=None):
    return lax.dot_general(a, b, (((1,), (1,)), ((), ())), preferred_element_type=F32, precision=precision)


def _mm(a, b, precision=None):
    return jnp.dot(a, b, preferred_element_type=F32, precision=precision)


def _rms(x, gain):
    return (x * lax.rsqrt(jnp.mean(x * x, axis=-1, keepdims=True) + NORM_EPS)) * gain


def _params(*sem):
    return pltpu.CompilerParams(dimension_semantics=sem, vmem_limit_bytes=VMEM_LIMIT)


_N_GROUPS = (("mk", 256, BF16), ("nks", 256, BF16), ("nkw", 128, BF16), ("dk", 256, BF16), ("sk", 128, BF16),
             ("ncmp", 128, F32))
_T_GROUPS = (("mq", 256, BF16), ("nq", 256, BF16), ("dq", 256, BF16), ("sq", 256, BF16),
             ("mv", 256, BF16), ("dv", 256, BF16), ("nvs", 64, BF16), ("nvw", 64, BF16), ("sv", 128, BF16),
             ("gates", 768, F32))
_NORM_ROWS = {"mk": (0, HEAD_DIM), "nks": (1, HEAD_DIM), "dk": (2, DIFF_QK_DIM)}
_ZERO_COL = 2700


def _in_columns():
    r = np.arange
    z = lambda n: np.full(n, _ZERO_COL)
    ncols = np.concatenate([r(256, 512), r(1152, 1216), z(192), r(1280, 1344), z(64), r(1676, 1932),
                            r(2444, 2572), r(1024, 1152)])
    gate = np.asarray([1408 + 3 * h + j for j in range(3) for h in range(4) for _ in range(64)])
    tcols = np.concatenate([r(0, 256), r(768, 1024), r(1420, 1676), r(2188, 2444), r(512, 768), r(1932, 2188),
                            r(1216, 1280), r(1344, 1408), r(2572, 2700), gate])
    assert ncols.shape[0] == sum(w for _, w, _ in _N_GROUPS) and tcols.shape[0] == sum(w for _, w, _ in _T_GROUPS)
    return ncols, tcols


def _in_proj_kernel(x_ref, g_ref, wn_ref, wt_ref, *out_refs, tm, seq):
    names = [n for n, _, _ in _N_GROUPS] + [n for n, _, _ in _T_GROUPS] + ["kmean", "knorm"]
    outs = dict(zip(names, out_refs))
    hb = _rms(x_ref[...], g_ref[...]).astype(BF16)
    off = 0
    for name, width, dtype in _N_GROUPS:
        res = _mm(hb, wn_ref[:, off:off + width])
        off += width
        if name == "mk":
            for r in range(tm // MOBA_BLOCK):
                blk = res[r * MOBA_BLOCK:(r + 1) * MOBA_BLOCK]
                outs["kmean"][0, r:r + 1, :] = jnp.mean(blk, axis=0, keepdims=True)
        if name in _NORM_ROWS:
            nrow, glanes = _NORM_ROWS[name]
            kb = res.astype(BF16).astype(F32)
            il = lax.broadcasted_iota(jnp.int32, (GROUP_WIDTH, 128), 0)
            ic = lax.broadcasted_iota(jnp.int32, (GROUP_WIDTH, 128), 1)
            nrm = _mm(kb * kb, jnp.where(il // glanes == ic, 1.0, 0.0), precision=HIGHEST)
            for c in range(tm // TQ):
                outs["knorm"][c, nrow:nrow + 1, :] = jnp.max(nrm[c * TQ:(c + 1) * TQ], axis=0, keepdims=True)
        if name == "nks":
            lane = lax.broadcasted_iota(jnp.int32, res.shape, 1)
            row = lax.broadcasted_iota(jnp.int32, res.shape, 0)
            pos = (pl.program_id(0) % (seq // tm)) * tm + row
            onehot = (lane - AUG_BLK0) == lax.shift_right_logical(pos, 6)
            feat = jnp.where((lane >= AUG_POS0) & (lane < AUG_ONE0), (row % TQ).astype(F32),
                             jnp.where((lane >= AUG_ONE0) & (lane < AUG_ONE0 + 3), 1.0, 0.0))
            res = jnp.where(lane < AUG_BLK0, res, jnp.where(lane < AUG_POS0, jnp.where(onehot, 1.0, 0.0), feat))
        outs[name][...] = res.astype(dtype)
    for c in range(tm // TQ):
        outs["knorm"][c, len(_NORM_ROWS):8, :] = jnp.zeros((8 - len(_NORM_ROWS), 128), F32)
    off = 0
    for name, width, dtype in _T_GROUPS:
        res = _nt(wt_ref[off:off + width, :], hb)
        off += width
        if name == "gates":
            res = 1.0 / (1.0 + jnp.exp(-res))
        for c in range(tm // TQ):
            outs[name][c] = res[:, c * TQ:(c + 1) * TQ].astype(dtype)


def _in_proj(x2, gain, wn, wt, seq):
    n, d = x2.shape
    tm = 512
    out_shape = [jax.ShapeDtypeStruct((n, wd), dt) for _, wd, dt in _N_GROUPS]
    out_specs = [pl.BlockSpec((tm, wd), lambda i: (i, 0)) for _, wd, _ in _N_GROUPS]
    out_shape += [jax.ShapeDtypeStruct((n // TQ, wd, TQ), dt) for _, wd, dt in _T_GROUPS]
    out_specs += [pl.BlockSpec((tm // TQ, wd, TQ), lambda i: (i, 0, 0)) for _, wd, _ in _T_GROUPS]
    out_shape.append(jax.ShapeDtypeStruct((n // tm, tm // MOBA_BLOCK, 256), F32))
    out_specs.append(pl.BlockSpec((1, tm // MOBA_BLOCK, 256), lambda i: (i, 0, 0)))
    out_shape.append(jax.ShapeDtypeStruct((n // TQ, 8, 128), F32))
    out_specs.append(pl.BlockSpec((tm // TQ, 8, 128), lambda i: (i, 0, 0)))
    outs = pl.pallas_call(
        functools.partial(_in_proj_kernel, tm=tm, seq=seq),
        grid=(n // tm,),
        in_specs=[pl.BlockSpec((tm, d), lambda i: (i, 0)),
                  pl.BlockSpec((1, d), lambda i: (0, 0)),
                  pl.BlockSpec(wn.shape, lambda i: (0, 0)),
                  pl.BlockSpec(wt.shape, lambda i: (0, 0))],
        out_specs=out_specs,
        out_shape=out_shape,
        compiler_params=_params("parallel"),
        name="in_proj",
    )(x2, gain.reshape(1, d), wn, wt)
    names = [nm for nm, _, _ in _N_GROUPS] + [nm for nm, _, _ in _T_GROUPS] + ["kmean", "knorm"]
    return dict(zip(names, outs))


def _rel_pos(tq):
    key = lax.broadcasted_iota(jnp.int32, (tq, tq), 0)
    qry = lax.broadcasted_iota(jnp.int32, (tq, tq), 1)
    return key, qry


def _head_rows(h, n=1):
    return slice(h * HEAD_DIM, (h + n) * HEAD_DIM)


def _flash_init(m_ref, acc_ref):
    m_ref[...] = jnp.full(m_ref.shape, NEG_INF, F32)
    acc_ref[...] = jnp.zeros(acc_ref.shape, F32)


def _with_ones(vt):
    return jnp.concatenate([vt, jnp.ones((ONES_ROWS, vt.shape[1]), BF16)], axis=0)


def _score(s_ref, c, sv):
    s_ref[c] = sv
    tk, tq = sv.shape
    return jnp.max(sv.reshape(tk // 8, 8, tq), axis=0)


def _flash_update(s_ref, c, mx8, cj, vt, idx, m_ref, acc_ref):
    tk = s_ref.shape[1]
    m_old = m_ref[idx:idx + 1, :]
    m_new = jnp.maximum(m_old, jnp.max(mx8, axis=0, keepdims=True) + cj)
    shift = m_new - cj
    ps = [jnp.exp2(s_ref[c, r * 64:(r + 1) * 64, :] - shift).astype(BF16) for r in range(tk // 64)]
    alpha = jnp.exp2(m_old - m_new)
    acc_ref[idx] = alpha * acc_ref[idx] + _mm(vt, jnp.concatenate(ps, axis=0))
    m_ref[idx:idx + 1, :] = m_new


def _reach_tiles(rhs, m_row, kn2, slope2, i, tq):
    rf = rhs.astype(F32)
    qn2 = jnp.max(jnp.sum(rf * rf, axis=0, keepdims=True), axis=1, keepdims=True)
    bound = jnp.sqrt(qn2 * kn2) * 1.01
    reach = (bound - jnp.min(m_row, axis=1, keepdims=True) + UNDERFLOW_MARGIN) * (1.0 / slope2)
    nb = jnp.floor((reach - 1.0) * (1.0 / tq)) + 1.0
    return jnp.minimum(jnp.maximum(nb, 0.0), i.astype(F32))


def _segmented_sweep(i, nback, make_stage):
    nh = len(nback)
    starts, prev = [], i
    for h in range(nh):
        st = jnp.minimum(i - nback[h].astype(jnp.int32)[0, 0], prev)
        starts.append(st)
        prev = st
    bounds = starts[::-1] + [i]
    for n in range(1, nh + 1):
        lo, hi = bounds[n - 1], bounds[n]
        score, absorb = make_stage(tuple(range(nh - n, nh)))

        def step(j0, size, score=score, absorb=absorb):
            mx = [score(j0 + t, t) for t in range(size)]
            for t in range(size):
                absorb(j0 + t, t, mx[t])

        def group(g, carry, lo=lo, step=step):
            step(lo + SWEEP_GROUP * g, SWEEP_GROUP)
            return carry

        ngroups = (hi - lo) // SWEEP_GROUP
        lax.fori_loop(0, ngroups, group, 0)
        rem, done = (hi - lo) % SWEEP_GROUP, lo + SWEEP_GROUP * ngroups
        size = SWEEP_GROUP // 2
        while size >= 1:
            @pl.when((rem // size) % 2 == 1)
            def _(size=size, start=done + (rem // (2 * size)) * (2 * size), step=step):
                step(start, size)
            size //= 2


def _normalized(acc_ref, idx):
    a = acc_ref[idx]
    return a[0:HEAD_DIM] / a[HEAD_DIM:HEAD_DIM + 1]


def _topk_bias_t(score_t, k):
    nrow = score_t.shape[0]
    rows = lax.broadcasted_iota(jnp.int32, score_t.shape, 0).astype(F32)
    bias = jnp.full(score_t.shape, NEG_INF, F32)
    work = score_t
    for _ in range(k):
        m = jnp.max(work, axis=0, keepdims=True)
        idx = jnp.min(jnp.where(work == m, rows, float(nrow)), axis=0, keepdims=True)
        pick = rows == idx
        bias = jnp.where(pick, jnp.where(m > 0.5 * NEG_INF, 0.0, NEG_INF), bias)
        work = jnp.where(pick, REMOVED, work)
    return bias


def _k_tile(ref, j, tk):
    return ref[0, pl.ds(pl.multiple_of(j * tk, tk), tk), :]


def _qt_spec(width, nq):
    return pl.BlockSpec((1, width, TQ), lambda b, i: (b * nq + i, 0, 0))


def _kn_spec(nq):
    return pl.BlockSpec((nq, 8, 128), lambda b, i: (b, 0, 0))


def _k_spec(s, width):
    return pl.BlockSpec((1, s, width), lambda b, i: (b, 0, 0))


def _vt_spec(nq, width):
    return pl.BlockSpec((nq, width, TQ), lambda b, i: (b, 0, 0))


def _row_spec(width, nq):
    return pl.BlockSpec((TQ, width), lambda b, i: (b * nq + i, 0))


def _flash_scratch(chains, tiles):
    return [pltpu.VMEM((8, TQ), F32), pltpu.VMEM((chains, HEAD_DIM + ONES_ROWS, TQ), F32),
            pltpu.VMEM((tiles, TQ, TQ), F32), pltpu.VMEM((GROUP_WIDTH, TQ), F32)]


def _moba_kernel(qt_ref, k_ref, vt_ref, km_ref, kn_ref, o_ref, rhs_ref, tab_ref, sel_ref, m_ref, acc_ref, s_ref, ot_ref,
                 *, slopes, tq):
    i = pl.program_id(1)
    nh = GROUP_HEADS
    scale = HEAD_DIM ** -0.5 * LOG2E
    qt = qt_ref[0].astype(F32)
    key, qry = _rel_pos(tq)
    rel = (qry - key).astype(F32)
    causal = key <= qry
    row_head = lax.broadcasted_iota(jnp.int32, (GROUP_WIDTH, 1), 0) // HEAD_DIM
    blk_rows = lax.broadcasted_iota(jnp.int32, (SEL_ROWS, tq), 0)
    _flash_init(m_ref, acc_ref)
    kd = _k_tile(k_ref, i, tq)
    for h in range(nh):
        qh = jnp.where(row_head == h, qt, 0.0)
        gate_t = _mm(km_ref[0], qh, precision=HIGHEST)
        gate_t = jnp.where(blk_rows < i, gate_t, NEG_INF)
        sel_ref[h] = _topk_bias_t(gate_t, MOBA_TOPK)
        rhs_ref[h] = (qh * scale).astype(BF16)
        tab_ref[h] = rel * (-slopes[h] * LOG2E)
    mx = [_score(s_ref, h, jnp.where(causal, _mm(kd, rhs_ref[h]) + tab_ref[h], NEG_INF)) for h in range(nh)]
    for h in range(nh):
        _flash_update(s_ref, h, mx[h], 0.0, _with_ones(vt_ref[i, _head_rows(h), :]), h, m_ref, acc_ref)

    kn = jnp.max(kn_ref[...], axis=0)
    nback = [_reach_tiles(rhs_ref[h], m_ref[h:h + 1, :], kn[0:1, h:h + 1], slopes[h] * LOG2E, i, tq)
             for h in range(nh)]

    def make_stage(heads):
        def score(j, slot):
            kj = _k_tile(k_ref, j, tq)
            return tuple(_score(s_ref, slot * nh + h, _mm(kj, rhs_ref[h]) + tab_ref[h]) for h in heads)

        def absorb(j, slot, mx):
            dj = (i - j).astype(F32)
            for n, h in enumerate(heads):
                cj = dj * (-slopes[h] * LOG2E * tq) + sel_ref[h, pl.ds(j, 1), :]
                _flash_update(s_ref, slot * nh + h, mx[n], cj, _with_ones(vt_ref[j, _head_rows(h), :]), h,
                              m_ref, acc_ref)
        return score, absorb

    _segmented_sweep(i, nback, make_stage)
    for h in range(nh):
        ot_ref[_head_rows(h), :] = _normalized(acc_ref, h)
    o_ref[...] = ot_ref[...].T.astype(o_ref.dtype)


def _moba(qt, k, vt, kmean, knorm, b, s):
    nq = s // TQ
    return pl.pallas_call(
        functools.partial(_moba_kernel, slopes=_alibi_slopes(0), tq=TQ),
        grid=(b, nq),
        in_specs=[_qt_spec(GROUP_WIDTH, nq), _k_spec(s, GROUP_WIDTH), _vt_spec(nq, GROUP_WIDTH),
                  pl.BlockSpec((1, SEL_ROWS, GROUP_WIDTH), lambda bb, i: (bb, 0, 0)), _kn_spec(nq)],
        out_specs=_row_spec(GROUP_WIDTH, nq),
        out_shape=jax.ShapeDtypeStruct((b * s, GROUP_WIDTH), BF16),
        scratch_shapes=[pltpu.VMEM((GROUP_HEADS, GROUP_WIDTH, TQ), BF16),
                        pltpu.VMEM((GROUP_HEADS, TQ, TQ), F32),
                        pltpu.VMEM((GROUP_HEADS, SEL_ROWS, TQ), F32)]
        + _flash_scratch(GROUP_HEADS, SWEEP_GROUP * GROUP_HEADS),
        compiler_params=_params("parallel", "arbitrary"),
        name="moba",
    )(qt, k.reshape(b, s, GROUP_WIDTH), vt, kmean, knorm)


def _nsa_cmp_kernel(x_ref, pa_ref, pb_ref, w1a_ref, w1b_ref, b1_ref, w2_ref, o_ref, *, transposed):
    x = x_ref[0]
    n = x.shape[0]
    a = _mm(x + pa_ref[...], w1a_ref[...], precision=HIGHEST)
    bm = _mm(x + pb_ref[...], w1b_ref[...], precision=HIGHEST)
    hid = jax.nn.gelu(a + pltpu.roll(bm, n - 1, 0) + b1_ref[...], approximate=True)
    if transposed:
        o_ref[0] = _nt(w2_ref[...], hid, precision=HIGHEST)
    else:
        o_ref[0] = _mm(hid, w2_ref[...], precision=HIGHEST)


def _nsa_compress(x, pos, w1, b1, w2, b, s, transposed):
    rows = s // NSA_CMP_STRIDE
    half = NSA_CMP_STRIDE * HEAD_DIM
    hid = w1.shape[-1]
    full = lambda shape: pl.BlockSpec(shape, lambda bb: (0,) * len(shape))
    if transposed:
        w2 = w2.T
        oshape = (b, HEAD_DIM, rows)
    else:
        w2 = jnp.pad(w2, ((0, 0), (0, 128 - HEAD_DIM)))
        oshape = (b, rows, 128)
    return pl.pallas_call(
        functools.partial(_nsa_cmp_kernel, transposed=transposed),
        grid=(b,),
        in_specs=[pl.BlockSpec((1, rows, half), lambda bb: (bb, 0, 0)),
                  full((1, half)), full((1, half)), full((half, hid)), full((half, hid)),
                  full((1, hid)), full(w2.shape)],
        out_specs=pl.BlockSpec((1,) + oshape[1:], lambda bb: (bb, 0, 0)),
        out_shape=jax.ShapeDtypeStruct(oshape, F32),
        compiler_params=_params("parallel"),
        name="nsa_compress",
    )(x.reshape(b, rows, half),
      pos[:NSA_CMP_STRIDE].reshape(1, half), pos[NSA_CMP_STRIDE:].reshape(1, half),
      w1[:NSA_CMP_STRIDE].reshape(half, hid), w1[NSA_CMP_STRIDE:].reshape(half, hid),
      b1.reshape(1, hid), w2)


def _nsa_sel_kernel(qt_ref, kc_ref, vct_ref, ov_ref, oc_ref, sel_ref, *, slopes, tq):
    i = pl.program_id(1)
    scale = HEAD_DIM ** -0.5
    nc = kc_ref.shape[1]
    qt = qt_ref[0].astype(F32)
    kc = kc_ref[0].astype(BF16)
    vct = vct_ref[0].astype(BF16)
    t = i * tq + lax.broadcasted_iota(jnp.int32, (nc, tq), 1)
    cmp_end = lax.broadcasted_iota(jnp.int32, (nc, tq), 0) * NSA_CMP_STRIDE + (NSA_CMP_LEN - 1)
    dist = (t - cmp_end).astype(F32)
    mask = cmp_end <= t
    zeros = jnp.zeros((HEAD_DIM, tq), BF16)
    p_sum = jnp.zeros((nc, tq), F32)
    for h in range(GROUP_HEADS):
        qs = (qt[_head_rows(h), :] * scale).astype(BF16)
        sc = _mm(kc, jnp.concatenate([qs, zeros], axis=0)) - slopes[h] * dist
        sc = jnp.where(mask, sc, NEG_INF)
        m = jnp.max(sc, axis=0, keepdims=True)
        p = jnp.where(mask, jnp.exp(sc - m), 0.0)
        p = p / jnp.maximum(jnp.sum(p, axis=0, keepdims=True), TINY)
        oc_ref[0, _head_rows(h), :] = _mm(vct, p.astype(BF16))
        p_sum = p_sum + p
    imp_t = _mm(ov_ref[...], p_sum, precision=HIGHEST)
    blk = lax.broadcasted_iota(jnp.int32, (SEL_ROWS, tq), 0)
    tt = i * tq + lax.broadcasted_iota(jnp.int32, (SEL_ROWS, tq), 1)
    qblk = lax.shift_right_logical(tt, 6)
    forced = (blk == 0) | (blk == qblk) | (blk == qblk - 1)
    score = jnp.where(forced, FORCED_SCORE, imp_t)
    score = jnp.where(blk * NSA_SEL_BLOCK <= tt, score, NEG_INF)
    sel_ref[0] = _topk_bias_t(score, NSA_SEL_TOPN).astype(sel_ref.dtype)


def _nsa_select(qt, kc, vct, b, s):
    nq = s // TQ
    nc = s // NSA_CMP_STRIDE
    nsel = s // NSA_SEL_BLOCK
    assert nsel <= SEL_ROWS and NSA_SEL_TOPN <= nsel
    cs = np.arange(nc)[None, :] * NSA_CMP_STRIDE
    bs = np.arange(SEL_ROWS)[:, None] * NSA_SEL_BLOCK
    ov = np.clip(np.minimum(cs + NSA_CMP_LEN, bs + NSA_SEL_BLOCK) - np.maximum(cs, bs), 0, None) / NSA_CMP_LEN
    ov[:, nc - 1] = 0.0
    ov[nsel:] = 0.0
    return pl.pallas_call(
        functools.partial(_nsa_sel_kernel, slopes=_alibi_slopes(1), tq=TQ),
        grid=(b, nq),
        in_specs=[_qt_spec(GROUP_WIDTH, nq),
                  pl.BlockSpec((1, nc, 128), lambda bb, i: (bb, 0, 0)),
                  pl.BlockSpec((1, HEAD_DIM, nc), lambda bb, i: (bb, 0, 0)),
                  pl.BlockSpec((SEL_ROWS, nc), lambda bb, i: (0, 0))],
        out_specs=[_qt_spec(GROUP_WIDTH, nq), _qt_spec(SEL_ROWS, nq)],
        out_shape=[jax.ShapeDtypeStruct((b * nq, GROUP_WIDTH, TQ), F32),
                   jax.ShapeDtypeStruct((b * nq, SEL_ROWS, TQ), BF16)],
        compiler_params=_params("parallel", "arbitrary"),
        name="nsa_select",
    )(qt, kc, vct, jnp.asarray(ov.astype(np.float32)))


def _nsa_attn_kernel(qt_ref, ks_ref, vst_ref, kw_ref, vwt_ref, sel_ref, oc_ref, g_ref, kn_ref, o_ref,
                     rhs_ref, rhsw_ref, tab_ref, m_ref, acc_ref, s_ref, ot_ref, *, slopes, tq):
    i = pl.program_id(1)
    scale = HEAD_DIM ** -0.5 * LOG2E
    nh = GROUP_HEADS
    qt = qt_ref[0].astype(F32)
    key, qry = _rel_pos(tq)
    rel = (qry - key).astype(F32)
    causal = key <= qry
    frow = lax.broadcasted_iota(jnp.int32, (HEAD_DIM, tq), 0)
    qoff = lax.broadcasted_iota(jnp.int32, (1, tq), 1).astype(F32)
    zeros = jnp.zeros((HEAD_DIM, tq), BF16)
    _flash_init(m_ref, acc_ref)
    ksd = _k_tile(ks_ref, i, tq)
    kwd = _k_tile(kw_ref, i, tq)
    for h in range(nh):
        qs = (qt[_head_rows(h), :] * scale).astype(BF16)
        slope2 = float(np.float32(slopes[h] * LOG2E))
        s1, s2, s3 = _bf16_pieces(slope2)
        t = qoff * (-slope2)
        t1 = t.astype(BF16).astype(F32)
        t2 = (t - t1).astype(BF16).astype(F32)
        t3 = t - t1 - t2
        feat = jnp.where(frow == 0, s1, jnp.where(frow == 1, s2, jnp.where(frow == 2, s3, jnp.where(
            frow == 3, t1, jnp.where(frow == 4, t2, jnp.where(frow == 5, t3, 0.0))))))
        rhs_ref[h, 0:AUG_BLK0, :] = qs
        rhs_ref[h, AUG_BLK0:AUG_POS0, :] = sel_ref[0]
        rhs_ref[h, AUG_POS0:GROUP_WIDTH, :] = feat.astype(BF16)
        rhsw_ref[h] = jnp.concatenate([qs, zeros], axis=0)
        tab_ref[h] = rel * (-slope2)
    vsd = _with_ones(vst_ref[i])
    vwd = _with_ones(vwt_ref[i])
    mx = [_score(s_ref, h, jnp.where(causal, _mm(ksd, rhs_ref[h]), NEG_INF)) for h in range(nh)]
    for h in range(nh):
        _flash_update(s_ref, h, mx[h], 0.0, vsd, h, m_ref, acc_ref)
    mx = [_score(s_ref, h, jnp.where(causal, _mm(kwd, rhsw_ref[h]) + tab_ref[h], NEG_INF)) for h in range(nh)]
    for h in range(nh):
        _flash_update(s_ref, h, mx[h], 0.0, vwd, nh + h, m_ref, acc_ref)

    kn = jnp.max(kn_ref[...], axis=0)
    nback = [_reach_tiles(rhs_ref[h, 0:AUG_BLK0, :], m_ref[h:h + 1, :], kn[1:2, 0:1], slopes[h] * LOG2E, i, tq)
             for h in range(nh)]

    def make_stage(heads):
        def score(j, slot):
            ksj = _k_tile(ks_ref, j, tq)
            return tuple(_score(s_ref, slot * nh + h, _mm(ksj, rhs_ref[h])) for h in heads)

        def absorb(j, slot, mx):
            vsj = _with_ones(vst_ref[j])
            dj = (i - j).astype(F32)
            for n, h in enumerate(heads):
                _flash_update(s_ref, slot * nh + h, mx[n], dj * (-slopes[h] * LOG2E * tq), vsj, h, m_ref, acc_ref)
        return score, absorb

    _segmented_sweep(i, nback, make_stage)

    for back in range(1, NSA_WINDOW // tq + 1):
        @pl.when(i >= back)
        def _(back=back):
            kwj = _k_tile(kw_ref, i - back, tq)
            vwj = _with_ones(vwt_ref[i - back])
            mx = []
            for h in range(nh):
                sv = _mm(kwj, rhsw_ref[h]) + tab_ref[h]
                if back * tq == NSA_WINDOW:
                    sv = jnp.where(key > qry, sv, NEG_INF)
                mx.append(_score(s_ref, h, sv))
            for h in range(nh):
                _flash_update(s_ref, h, mx[h], -slopes[h] * LOG2E * tq * back, vwj, nh + h, m_ref, acc_ref)

    for h in range(nh):
        rows = _head_rows(h)
        ot_ref[rows, :] = (g_ref[0, rows, :] * oc_ref[0, rows, :]
                           + g_ref[0, _head_rows(nh + h), :] * _normalized(acc_ref, h)
                           + g_ref[0, _head_rows(2 * nh + h), :] * _normalized(acc_ref, nh + h))
    o_ref[...] = ot_ref[...].T.astype(o_ref.dtype)


def _nsa_attend(qt, ks, vst, kw, vwt, sel, oct, gt, knorm, b, s):
    nq = s // TQ
    return pl.pallas_call(
        functools.partial(_nsa_attn_kernel, slopes=_alibi_slopes(1), tq=TQ),
        grid=(b, nq),
        in_specs=[_qt_spec(GROUP_WIDTH, nq), _k_spec(s, GROUP_WIDTH), _vt_spec(nq, HEAD_DIM),
                  _k_spec(s, 128), _vt_spec(nq, HEAD_DIM), _qt_spec(SEL_ROWS, nq), _qt_spec(GROUP_WIDTH, nq),
                  _qt_spec(3 * GROUP_WIDTH, nq), _kn_spec(nq)],
        out_specs=_row_spec(GROUP_WIDTH, nq),
        out_shape=jax.ShapeDtypeStruct((b * s, GROUP_WIDTH), BF16),
        scratch_shapes=[pltpu.VMEM((GROUP_HEADS, GROUP_WIDTH, TQ), BF16),
                        pltpu.VMEM((GROUP_HEADS, 128, TQ), BF16),
                        pltpu.VMEM((GROUP_HEADS, TQ, TQ), F32)]
        + _flash_scratch(2 * GROUP_HEADS, SWEEP_GROUP * GROUP_HEADS),
        compiler_params=_params("parallel", "arbitrary"),
        name="nsa_attend",
    )(qt, ks.reshape(b, s, GROUP_WIDTH), vst, kw.reshape(b, s, 128), vwt, sel, oct, gt, knorm)


def _diff_kernel(qt_ref, k_ref, vt_ref, lam_ref, sub_ref, kn_ref, o_ref, rhs_ref, tab_ref, m_ref, acc_ref, s_ref, ot_ref,
                 *, slopes, tq, lambda_init):
    i = pl.program_id(1)
    scale = DIFF_QK_DIM ** -0.5 * LOG2E
    nh = GROUP_HEADS
    qt = qt_ref[0].astype(F32)
    key, qry = _rel_pos(tq)
    rel = (qry - key).astype(F32)
    causal = key <= qry
    row = lax.broadcasted_iota(jnp.int32, (GROUP_WIDTH, 1), 0)
    _flash_init(m_ref, acc_ref)
    kd = _k_tile(k_ref, i, tq)
    for h in range(nh):
        lo = h * HEAD_DIM
        rhs_ref[h] = (jnp.where((row >= lo) & (row < lo + DIFF_QK_DIM), qt, 0.0) * scale).astype(BF16)
        rhs_ref[nh + h] = (jnp.where((row >= lo + DIFF_QK_DIM) & (row < lo + HEAD_DIM), qt, 0.0)
                           * scale).astype(BF16)
        tab_ref[h] = rel * (-slopes[h] * LOG2E)
    mx = [_score(s_ref, c, jnp.where(causal, _mm(kd, rhs_ref[c]) + tab_ref[c % nh], NEG_INF))
          for c in range(2 * nh)]
    for c in range(2 * nh):
        _flash_update(s_ref, c, mx[c], 0.0, _with_ones(vt_ref[i, _head_rows(c % nh), :]), c, m_ref, acc_ref)

    kn = jnp.max(kn_ref[...], axis=0)
    nback = [jnp.maximum(
        _reach_tiles(rhs_ref[h], m_ref[h:h + 1, :], kn[2:3, 2 * h:2 * h + 1], slopes[h] * LOG2E, i, tq),
        _reach_tiles(rhs_ref[nh + h], m_ref[nh + h:nh + h + 1, :], kn[2:3, 2 * h + 1:2 * h + 2],
                     slopes[h] * LOG2E, i, tq)) for h in range(nh)]

    def make_stage(heads):
        chains = tuple(heads) + tuple(nh + h for h in heads)

        def score(j, slot):
            kj = _k_tile(k_ref, j, tq)
            return tuple(_score(s_ref, slot * 2 * nh + c, _mm(kj, rhs_ref[c]) + tab_ref[c % nh]) for c in chains)

        def absorb(j, slot, mx):
            dj = (i - j).astype(F32)
            for n, c in enumerate(chains):
                h = c % nh
                _flash_update(s_ref, slot * 2 * nh + c, mx[n], dj * (-slopes[h] * LOG2E * tq),
                              _with_ones(vt_ref[j, _head_rows(h), :]), c, m_ref, acc_ref)
        return score, absorb

    _segmented_sweep(i, nback, make_stage)
    lam_p = lam_ref[...]
    lam = (jnp.exp(jnp.sum(lam_p[0:1] * lam_p[1:2], axis=-1, keepdims=True))
           - jnp.exp(jnp.sum(lam_p[2:3] * lam_p[3:4], axis=-1, keepdims=True)) + lambda_init)
    for h in range(nh):
        o = _normalized(acc_ref, h) - lam * _normalized(acc_ref, nh + h)
        ms = jnp.mean(o * o, axis=0, keepdims=True)
        ot_ref[_head_rows(h), :] = ((o * lax.rsqrt(ms + NORM_EPS)) * sub_ref[...]) * (1.0 - lambda_init)
    o_ref[...] = ot_ref[...].T.astype(o_ref.dtype)


def _diff(qt, k, vt, lam, subln, lambda_init, knorm, b, s):
    nq = s // TQ
    return pl.pallas_call(
        functools.partial(_diff_kernel, slopes=_alibi_slopes(2), tq=TQ, lambda_init=lambda_init),
        grid=(b, nq),
        in_specs=[_qt_spec(GROUP_WIDTH, nq), _k_spec(s, GROUP_WIDTH), _vt_spec(nq, GROUP_WIDTH),
                  pl.BlockSpec((4, DIFF_QK_DIM), lambda bb, i: (0, 0)),
                  pl.BlockSpec((HEAD_DIM, TQ), lambda bb, i: (0, 0)), _kn_spec(nq)],
        out_specs=_row_spec(GROUP_WIDTH, nq),
        out_shape=jax.ShapeDtypeStruct((b * s, GROUP_WIDTH), BF16),
        scratch_shapes=[pltpu.VMEM((2 * GROUP_HEADS, GROUP_WIDTH, TQ), BF16),
                        pltpu.VMEM((GROUP_HEADS, TQ, TQ), F32)]
        + _flash_scratch(2 * GROUP_HEADS, SWEEP_GROUP * 2 * GROUP_HEADS),
        compiler_params=_params("parallel", "arbitrary"),
        name="diff_attn",
    )(qt, k.reshape(b, s, GROUP_WIDTH), vt, lam, jnp.broadcast_to(subln[:, None], (HEAD_DIM, TQ)), knorm)


def _swa_kernel(qt_ref, k_ref, vt_ref, sink_ref, o_ref, rhs_ref, tab_ref, m_ref, acc_ref, s_ref, ot_ref,
                *, slopes, tq):
    i = pl.program_id(1)
    scale = HEAD_DIM ** -0.5 * LOG2E
    nh = GROUP_HEADS
    per_kv = nh // SWA_KV_HEADS
    qt = qt_ref[0].astype(F32)
    key, qry = _rel_pos(tq)
    d = qry - key
    rel = d.astype(F32)
    zeros = jnp.zeros((HEAD_DIM, tq), BF16)
    _flash_init(m_ref, acc_ref)
    kd = _k_tile(k_ref, i, tq)
    for h in range(nh):
        qs = (qt[_head_rows(h), :] * scale).astype(BF16)
        rhs_ref[h] = jnp.concatenate([qs, zeros] if h // per_kv == 0 else [zeros, qs], axis=0)
        tab_ref[h] = rel * (-slopes[h] * LOG2E)
    in_window = (d >= 0) & (d < SWA_WINDOW)
    mx = [_score(s_ref, h, jnp.where(in_window, _mm(kd, rhs_ref[h]) + tab_ref[h], NEG_INF)) for h in range(nh)]
    for h in range(nh):
        _flash_update(s_ref, h, mx[h], 0.0, _with_ones(vt_ref[i, _head_rows(h // per_kv), :]), h, m_ref, acc_ref)

    @pl.when(i >= 1)
    def _():
        kj = _k_tile(k_ref, i - 1, tq)
        mx = [_score(s_ref, h, jnp.where(d + tq < SWA_WINDOW, _mm(kj, rhs_ref[h]) + tab_ref[h], NEG_INF))
              for h in range(nh)]
        for h in range(nh):
            _flash_update(s_ref, h, mx[h], -slopes[h] * LOG2E * tq,
                          _with_ones(vt_ref[i - 1, _head_rows(h // per_kv), :]), h, m_ref, acc_ref)

    for h in range(nh):
        sink = sink_ref[h:h + 1, 0:1] * LOG2E
        m_old = m_ref[h:h + 1, :]
        m_new = jnp.maximum(m_old, sink)
        alpha = jnp.exp2(m_old - m_new)
        a = acc_ref[h]
        denom = alpha * a[HEAD_DIM:HEAD_DIM + 1] + jnp.exp2(sink - m_new)
        ot_ref[_head_rows(h), :] = (alpha * a[0:HEAD_DIM]) / denom
    o_ref[...] = ot_ref[...].T.astype(o_ref.dtype)


def _swa(qt, k, vt, sinks, b, s):
    nq = s // TQ
    kvw = SWA_KV_HEADS * HEAD_DIM
    return pl.pallas_call(
        functools.partial(_swa_kernel, slopes=_alibi_slopes(3), tq=TQ),
        grid=(b, nq),
        in_specs=[_qt_spec(GROUP_WIDTH, nq), _k_spec(s, kvw), _vt_spec(nq, kvw),
                  pl.BlockSpec((8, 128), lambda bb, i: (0, 0))],
        out_specs=_row_spec(GROUP_WIDTH, nq),
        out_shape=jax.ShapeDtypeStruct((b * s, GROUP_WIDTH), BF16),
        scratch_shapes=[pltpu.VMEM((GROUP_HEADS, kvw, TQ), BF16),
                        pltpu.VMEM((GROUP_HEADS, TQ, TQ), F32)] + _flash_scratch(GROUP_HEADS, GROUP_HEADS),
        compiler_params=_params("parallel", "arbitrary"),
        name="swa",
    )(qt, k.reshape(b, s, kvw), vt, jnp.pad(jnp.broadcast_to(sinks[:, None], (GROUP_HEADS, 128)), ((0, 4), (0, 0))))


def _out_proj_kernel(x_ref, a_ref, b_ref, c_ref, d_ref, w_ref, g_ref, o_ref):
    y = None
    for g, m_ref in enumerate((a_ref, b_ref, c_ref, d_ref)):
        part = _mm(m_ref[...], w_ref[g * GROUP_WIDTH:(g + 1) * GROUP_WIDTH, :])
        y = part if y is None else y + part
    o_ref[...] = x_ref[...] + _rms(y, g_ref[...])


def _out_proj(x2, mixes, w, gain):
    n, d = x2.shape
    tm = 512
    row = lambda wd: pl.BlockSpec((tm, wd), lambda i: (i, 0))
    return pl.pallas_call(
        _out_proj_kernel,
        grid=(n // tm,),
        in_specs=[row(d)] + [row(GROUP_WIDTH)] * 4 + [pl.BlockSpec((d, d), lambda i: (0, 0)),
                                                     pl.BlockSpec((1, d), lambda i: (0, 0))],
        out_specs=row(d),
        out_shape=jax.ShapeDtypeStruct((n, d), F32),
        compiler_params=_params("parallel"),
        name="out_proj",
    )(x2, *mixes, w, gain.reshape(1, d))


def _ffn_kernel(x_ref, xp_ref, gpre_ref, wg_ref, wu_ref, cw_ref, cb_ref, wd_ref, gpost_ref, o_ref,
                h_ref, hp_ref, acc_ref, *, tiles_per_seq):
    i = pl.program_id(0)
    f = pl.program_id(1)

    @pl.when(f == 0)
    def _():
        h_ref[...] = _rms(x_ref[...], gpre_ref[...]).astype(BF16)
        hp = _rms(xp_ref[...], gpre_ref[...])
        hp_ref[...] = jnp.where(i % tiles_per_seq != 0, hp, 0.0).astype(BF16)
        acc_ref[...] = jnp.zeros(acc_ref.shape, F32)

    h = h_ref[...]
    a = _mm(h, wg_ref[...])
    ap = _mm(hp_ref[...], wg_ref[...])
    row = lax.broadcasted_iota(jnp.int32, a.shape, 0)
    a1 = jnp.where(row == 0, ap[7:8], pltpu.roll(a, 1, 0))
    a2 = jnp.where(row == 0, ap[6:7], jnp.where(row == 1, ap[7:8], pltpu.roll(a, 2, 0)))
    cw = cw_ref[...]
    conv = cw[0:1] * a2 + cw[1:2] * a1 + cw[2:3] * a + cb_ref[...]
    gated = jax.nn.gelu(conv, approximate=True) * _mm(h, wu_ref[...])
    acc_ref[...] += _mm(gated.astype(BF16), wd_ref[...])

    @pl.when(f == pl.num_programs(1) - 1)
    def _():
        o_ref[...] = x_ref[...] + _rms(acc_ref[...], gpost_ref[...])


def _ffn(x2, gpre, wg, wu, cw, cb, wd, gpost, s):
    n, d = x2.shape
    dff = wg.shape[1]
    tm, tf = 1024, 512
    halo = 8
    return pl.pallas_call(
        functools.partial(_ffn_kernel, tiles_per_seq=s // tm),
        grid=(n // tm, dff // tf),
        in_specs=[pl.BlockSpec((tm, d), lambda i, f: (i, 0)),
                  pl.BlockSpec((halo, d), lambda i, f: (jnp.maximum(i * (tm // halo) - 1, 0), 0)),
                  pl.BlockSpec((1, d), lambda i, f: (0, 0)),
                  pl.BlockSpec((d, tf), lambda i, f: (0, f)),
                  pl.BlockSpec((d, tf), lambda i, f: (0, f)),
                  pl.BlockSpec((CONV_WIDTH, tf), lambda i, f: (0, f)),
                  pl.BlockSpec((1, tf), lambda i, f: (0, f)),
                  pl.BlockSpec((tf, d), lambda i, f: (f, 0)),
                  pl.BlockSpec((1, d), lambda i, f: (0, 0))],
        out_specs=pl.BlockSpec((tm, d), lambda i, f: (i, 0)),
        out_shape=jax.ShapeDtypeStruct((n, d), F32),
        scratch_shapes=[pltpu.VMEM((tm, d), BF16), pltpu.VMEM((halo, d), BF16), pltpu.VMEM((tm, d), F32)],
        compiler_params=_params("parallel", "arbitrary"),
        name="conv_ffn",
    )(x2, x2, gpre.reshape(1, d), wg, wu, cw, cb.reshape(1, dff), wd, gpost.reshape(1, d))


def kernel(x, attn_pre_norm, attn_post_norm, ffn_pre_norm, ffn_post_norm, w_in, w_out, nsa_cmp_pos_k, nsa_cmp_w1_k, nsa_cmp_b1_k, nsa_cmp_w2_k, nsa_cmp_pos_v, nsa_cmp_w1_v, nsa_cmp_b1_v, nsa_cmp_w2_v, diff_lambda_q1, diff_lambda_k1, diff_lambda_q2, diff_lambda_k2, diff_subln, swa_sinks, ffn_w_gate, ffn_w_up, ffn_conv_w, ffn_conv_b, ffn_w_down):
    b, s, d = x.shape
    depth = w_in.shape[0]
    assert s % 1024 == 0 and s // MOBA_BLOCK <= SEL_ROWS
    ncols, tcols = (jnp.asarray(c) for c in _in_columns())
    x2 = x.reshape(b * s, d)
    for l in range(depth):
        w_ext = jnp.pad(w_in[l], ((0, 0), (0, 1)))
        p = _in_proj(x2, attn_pre_norm[l], jnp.take(w_ext, ncols, axis=1).astype(BF16),
                     jnp.take(w_ext, tcols, axis=1).T.astype(BF16), s)
        kmean = p["kmean"].reshape(b, s // MOBA_BLOCK, GROUP_WIDTH)
        kmean = jnp.pad(kmean, ((0, 0), (0, SEL_ROWS - s // MOBA_BLOCK), (0, 0)))
        o_moba = _moba(p["mq"], p["mk"], p["mv"], kmean, p["knorm"], b, s)
        kc = _nsa_compress(p["ncmp"][:, :HEAD_DIM], nsa_cmp_pos_k[l], nsa_cmp_w1_k[l], nsa_cmp_b1_k[l],
                           nsa_cmp_w2_k[l], b, s, False)
        vct = _nsa_compress(p["ncmp"][:, HEAD_DIM:], nsa_cmp_pos_v[l], nsa_cmp_w1_v[l], nsa_cmp_b1_v[l],
                            nsa_cmp_w2_v[l], b, s, True)
        o_cmp, sel = _nsa_select(p["nq"], kc, vct, b, s)
        o_nsa = _nsa_attend(p["nq"], p["nks"], p["nvs"], p["nkw"], p["nvw"], sel, o_cmp, p["gates"],
                            p["knorm"], b, s)
        lam = jnp.stack([diff_lambda_q1[l], diff_lambda_k1[l], diff_lambda_q2[l], diff_lambda_k2[l]])
        lambda_init = 0.8 - 0.6 * math.exp(-0.3 * l)
        o_diff = _diff(p["dq"], p["dk"], p["dv"], lam, diff_subln[l], lambda_init, p["knorm"], b, s)
        o_swa = _swa(p["sq"], p["sk"], p["sv"], swa_sinks[l], b, s)
        x2 = _out_proj(x2, (o_moba, o_nsa, o_diff, o_swa), w_out[l].astype(BF16), attn_post_norm[l])
        x2 = _ffn(x2, ffn_pre_norm[l], ffn_w_gate[l].astype(BF16), ffn_w_up[l].astype(BF16), ffn_conv_w[l],
                  ffn_conv_b[l], ffn_w_down[l].astype(BF16), ffn_post_norm[l], s)
    return x2.reshape(b, s, d)
```

```python
import functools
import math

import numpy as np
import jax
import jax.numpy as jnp
from jax import lax
from jax.experimental import pallas as pl
from jax.experimental.pallas import tpu as pltpu

F32 = jnp.float32
BF16 = jnp.bfloat16
HIGHEST = lax.Precision.HIGHEST

N_MIXERS = 4
N_HEADS = 16
GROUP_HEADS = 4
HEAD_DIM = 64
GROUP_WIDTH = 256
MOBA_BLOCK = 256
MOBA_TOPK = 3
NSA_CMP_LEN = 32
NSA_CMP_STRIDE = 16
NSA_SEL_BLOCK = 64
NSA_SEL_TOPN = 16
NSA_WINDOW = 512
DIFF_QK_DIM = 32
SWA_KV_HEADS = 2
SWA_WINDOW = 128
CONV_WIDTH = 3
NORM_EPS = 1e-6
NEG_INF = -1e30
TINY = 1e-30
FORCED_SCORE = 1e4
LOG2E = 1.4426950408889634
ONES_ROWS = 16
UNDERFLOW_MARGIN = 160.0
REMOVED = -3e38

TQ = 256
SWEEP_GROUP = 4
SEL_ROWS = 128
MOBA_ROWS = 32
VMEM_LIMIT = 56 * 1024 * 1024

AUG_BLK0 = HEAD_DIM
AUG_POS0 = AUG_BLK0 + SEL_ROWS
AUG_ONE0 = AUG_POS0 + 3


def _alibi_slopes(mixer):
    slopes = np.power(np.float32(2.0), np.arange(1, N_HEADS + 1, dtype=np.float32) * np.float32(-8.0 / N_HEADS))
    return [float(s) for s in slopes[mixer::N_MIXERS]]


def _bf16_pieces(x):
    out, rem = [], np.float32(x)
    for _ in range(3):
        p = np.float32(np.asarray(rem, np.float32).astype(jnp.bfloat16).astype(np.float32))
        out.append(float(p))
        rem = np.float32(rem - p)
    assert rem == 0.0
    return out


def _nt(a, b, precision=None):
    return lax.dot_general(a, b, (((1,), (1,)), ((), ())), preferred_element_type=F32, precision=precision)


def _mm(a, b, precision=None):
    return jnp.dot(a, b, preferred_element_type=F32, precision=precision)


def _rms(x, gain):
    return (x * lax.rsqrt(jnp.mean(x * x, axis=-1, keepdims=True) + NORM_EPS)) * gain


def _params(*sem):
    return pltpu.CompilerParams(dimension_semantics=sem, vmem_limit_bytes=VMEM_LIMIT)


_N_GROUPS = (("mk", 256, BF16), ("nks", 256, BF16), ("nkw", 128, BF16), ("dk", 256, BF16), ("sk", 128, BF16),
             ("ncmp", 128, F32))
_T_GROUPS = (("mq", 256, BF16), ("nq", 256, BF16), ("dq", 256, BF16), ("sq", 256, BF16),
             ("mv", 256, BF16), ("dv", 256, BF16), ("nvs", 64, BF16), ("nvw", 64, BF16), ("sv", 128, BF16),
             ("gates", 768, F32))
_NORM_ROWS = {"mk": (0, HEAD_DIM), "nks": (1, HEAD_DIM), "dk": (2, DIFF_QK_DIM)}
_ZERO_COL = 2700


def _in_columns():
    r = np.arange
    z = lambda n: np.full(n, _ZERO_COL)
    ncols = np.concatenate([r(256, 512), r(1152, 1216), z(192), r(1280, 1344), z(64), r(1676, 1932),
                            r(2444, 2572), r(1024, 1152)])
    gate = np.asarray([1408 + 3 * h + j for j in range(3) for h in range(4) for _ in range(64)])
    tcols = np.concatenate([r(0, 256), r(768, 1024), r(1420, 1676), r(2188, 2444), r(512, 768), r(1932, 2188),
                            r(1216, 1280), r(1344, 1408), r(2572, 2700), gate])
    assert ncols.shape[0] == sum(w for _, w, _ in _N_GROUPS) and tcols.shape[0] == sum(w for _, w, _ in _T_GROUPS)
    return ncols, tcols


def _in_proj_kernel(x_ref, g_ref, wn_ref, wt_ref, *out_refs, tm, seq):
    names = [n for n, _, _ in _N_GROUPS] + [n for n, _, _ in _T_GROUPS] + ["kmean", "knorm"]
    outs = dict(zip(names, out_refs))
    hb = _rms(x_ref[...], g_ref[...]).astype(BF16)
    off = 0
    for name, width, dtype in _N_GROUPS:
        res = _mm(hb, wn_ref[:, off:off + width])
        off += width
        if name == "mk":
            for r in range(tm // MOBA_BLOCK):
                blk = res[r * MOBA_BLOCK:(r + 1) * MOBA_BLOCK]
                outs["kmean"][0, r:r + 1, :] = jnp.mean(blk, axis=0, keepdims=True)
        if name in _NORM_ROWS:
            nrow, glanes = _NORM_ROWS[name]
            kb = res.astype(BF16).astype(F32)
            il = lax.broadcasted_iota(jnp.int32, (GROUP_WIDTH, 128), 0)
            ic = lax.broadcasted_iota(jnp.int32, (GROUP_WIDTH, 128), 1)
            nrm = _mm((kb * kb).astype(BF16), jnp.where(il // glanes == ic, 1.0, 0.0).astype(BF16))
            for c in range(tm // TQ):
                outs["knorm"][c, nrow:nrow + 1, :] = jnp.max(nrm[c * TQ:(c + 1) * TQ], axis=0, keepdims=True)
        if name == "nks":
            lane = lax.broadcasted_iota(jnp.int32, res.shape, 1)
            row = lax.broadcasted_iota(jnp.int32, res.shape, 0)
            pos = (pl.program_id(0) % (seq // tm)) * tm + row
            onehot = (lane - AUG_BLK0) == lax.shift_right_logical(pos, 6)
            feat = jnp.where((lane >= AUG_POS0) & (lane < AUG_ONE0), (row % TQ).astype(F32),
                             jnp.where((lane >= AUG_ONE0) & (lane < AUG_ONE0 + 3), 1.0, 0.0))
            res = jnp.where(lane < AUG_BLK0, res, jnp.where(lane < AUG_POS0, jnp.where(onehot, 1.0, 0.0), feat))
        outs[name][...] = res.astype(dtype)
    for c in range(tm // TQ):
        outs["knorm"][c, len(_NORM_ROWS):8, :] = jnp.zeros((8 - len(_NORM_ROWS), 128), F32)
    off = 0
    for name, width, dtype in _T_GROUPS:
        res = _nt(wt_ref[off:off + width, :], hb)
        off += width
        if name == "gates":
            res = 1.0 / (1.0 + jnp.exp(-res))
        for c in range(tm // TQ):
            outs[name][c] = res[:, c * TQ:(c + 1) * TQ].astype(dtype)


def _in_proj(x2, gain, wn, wt, seq):
    n, d = x2.shape
    tm = 512
    out_shape = [jax.ShapeDtypeStruct((n, wd), dt) for _, wd, dt in _N_GROUPS]
    out_specs = [pl.BlockSpec((tm, wd), lambda i: (i, 0)) for _, wd, _ in _N_GROUPS]
    out_shape += [jax.ShapeDtypeStruct((n // TQ, wd, TQ), dt) for _, wd, dt in _T_GROUPS]
    out_specs += [pl.BlockSpec((tm // TQ, wd, TQ), lambda i: (i, 0, 0)) for _, wd, _ in _T_GROUPS]
    out_shape.append(jax.ShapeDtypeStruct((n // tm, tm // MOBA_BLOCK, 256), F32))
    out_specs.append(pl.BlockSpec((1, tm // MOBA_BLOCK, 256), lambda i: (i, 0, 0)))
    out_shape.append(jax.ShapeDtypeStruct((n // TQ, 8, 128), F32))
    out_specs.append(pl.BlockSpec((tm // TQ, 8, 128), lambda i: (i, 0, 0)))
    outs = pl.pallas_call(
        functools.partial(_in_proj_kernel, tm=tm, seq=seq),
        grid=(n // tm,),
        in_specs=[pl.BlockSpec((tm, d), lambda i: (i, 0)),
                  pl.BlockSpec((1, d), lambda i: (0, 0)),
                  pl.BlockSpec(wn.shape, lambda i: (0, 0)),
                  pl.BlockSpec(wt.shape, lambda i: (0, 0))],
        out_specs=out_specs,
        out_shape=out_shape,
        compiler_params=_params("parallel"),
        name="in_proj",
    )(x2, gain.reshape(1, d), wn, wt)
    names = [nm for nm, _, _ in _N_GROUPS] + [nm for nm, _, _ in _T_GROUPS] + ["kmean", "knorm"]
    return dict(zip(names, outs))


def _rel_pos(tq):
    key = lax.broadcasted_iota(jnp.int32, (tq, tq), 0)
    qry = lax.broadcasted_iota(jnp.int32, (tq, tq), 1)
    return key, qry


def _head_rows(h, n=1):
    return slice(h * HEAD_DIM, (h + n) * HEAD_DIM)


def _flash_init(m_ref, acc_ref):
    m_ref[...] = jnp.full(m_ref.shape, NEG_INF, F32)
    acc_ref[...] = jnp.zeros(acc_ref.shape, F32)


def _with_ones(vt):
    return jnp.concatenate([vt, jnp.ones((ONES_ROWS, vt.shape[1]), BF16)], axis=0)


def _score(s_ref, c, sv):
    s_ref[c] = sv
    tk, tq = sv.shape
    return jnp.max(sv.reshape(tk // 8, 8, tq), axis=0)


def _flash_update(s_ref, c, mx8, cj, vt, idx, m_ref, acc_ref):
    tk = s_ref.shape[1]
    m_old = m_ref[idx:idx + 1, :]
    m_new = jnp.maximum(m_old, jnp.max(mx8, axis=0, keepdims=True) + cj)
    shift = m_new - cj
    ps = [jnp.exp2(s_ref[c, r * 64:(r + 1) * 64, :] - shift).astype(BF16) for r in range(tk // 64)]
    alpha = jnp.exp2(m_old - m_new)
    acc_ref[idx] = alpha * acc_ref[idx] + _mm(vt, jnp.concatenate(ps, axis=0))
    m_ref[idx:idx + 1, :] = m_new


def _reach_tiles(rhs, m_row, kn2, slope2, i, tq):
    rf = rhs.astype(F32)
    qn2 = jnp.max(jnp.sum(rf * rf, axis=0, keepdims=True), axis=1, keepdims=True)
    bound = jnp.sqrt(qn2 * kn2) * 1.02
    reach = (bound - jnp.min(m_row, axis=1, keepdims=True) + UNDERFLOW_MARGIN) * (1.0 / slope2)
    nb = jnp.floor((reach - 1.0) * (1.0 / tq)) + 1.0
    return jnp.minimum(jnp.maximum(nb, 0.0), i.astype(F32))


def _segmented_sweep(i, nback, make_stage, staged_tiles):
    nh = len(nback)
    starts, prev = [], i
    for h in range(nh):
        st = jnp.minimum(i - nback[h].astype(jnp.int32)[0, 0], prev)
        starts.append(st)
        prev = st
    bounds = starts[::-1] + [i]
    for n in range(1, nh + 1):
        lo, hi = bounds[n - 1], bounds[n]
        score, absorb, per_slot = make_stage(tuple(range(nh - n, nh)))

        def step(j0, size, score=score, absorb=absorb):
            mx = [score(j0 + t, t) for t in range(size)]
            for t in range(size):
                absorb(j0 + t, t, mx[t])

        gsz = SWEEP_GROUP * 2 if n == 1 else SWEEP_GROUP
        assert gsz * per_slot <= staged_tiles

        def group(g, carry, lo=lo, step=step, gsz=gsz):
            step(lo + gsz * g, gsz)
            return carry

        ngroups = (hi - lo) // gsz
        lax.fori_loop(0, ngroups, group, 0)
        rem, done = (hi - lo) % gsz, lo + gsz * ngroups
        size = gsz // 2
        while size >= 1:
            @pl.when((rem // size) % 2 == 1)
            def _(size=size, start=done + (rem // (2 * size)) * (2 * size), step=step):
                step(start, size)
            size //= 2


def _normalized(acc_ref, idx):
    a = acc_ref[idx]
    return a[0:HEAD_DIM] / a[HEAD_DIM:HEAD_DIM + 1]


def _topk_bias_t(score_t, k):
    nrow = score_t.shape[0]
    rows = lax.broadcasted_iota(jnp.int32, score_t.shape, 0).astype(F32)
    bias = jnp.full(score_t.shape, NEG_INF, F32)
    work = score_t
    for _ in range(k):
        m = jnp.max(work, axis=0, keepdims=True)
        idx = jnp.min(jnp.where(work == m, rows, float(nrow)), axis=0, keepdims=True)
        pick = rows == idx
        bias = jnp.where(pick, jnp.where(m > 0.5 * NEG_INF, 0.0, NEG_INF), bias)
        work = jnp.where(pick, REMOVED, work)
    return bias


def _k_tile(ref, j, tk):
    return ref[0, pl.ds(pl.multiple_of(j * tk, tk), tk), :]


def _qt_spec(width, nq):
    return pl.BlockSpec((1, width, TQ), lambda b, i: (b * nq + i, 0, 0))


def _kn_spec(nq):
    return pl.BlockSpec((nq, 8, 128), lambda b, i: (b, 0, 0))


def _k_spec(s, width):
    return pl.BlockSpec((1, s, width), lambda b, i: (b, 0, 0))


def _vt_spec(nq, width):
    return pl.BlockSpec((nq, width, TQ), lambda b, i: (b, 0, 0))


def _row_spec(width, nq):
    return pl.BlockSpec((TQ, width), lambda b, i: (b * nq + i, 0))


def _flash_scratch(chains, tiles):
    return [pltpu.VMEM((8, TQ), F32), pltpu.VMEM((chains, HEAD_DIM + ONES_ROWS, TQ), F32),
            pltpu.VMEM((tiles, TQ, TQ), F32), pltpu.VMEM((GROUP_WIDTH, TQ), F32)]


def _moba_kernel(qt_ref, k_ref, vt_ref, km_ref, kn_ref, o_ref, rhs_ref, tab_ref, sel_ref, m_ref, acc_ref, s_ref, ot_ref,
                 *, slopes, tq):
    i = pl.program_id(1)
    nh = GROUP_HEADS
    scale = HEAD_DIM ** -0.5 * LOG2E
    qt = qt_ref[0].astype(F32)
    key, qry = _rel_pos(tq)
    rel = (qry - key).astype(F32)
    causal = key <= qry
    row_head = lax.broadcasted_iota(jnp.int32, (GROUP_WIDTH, 1), 0) // HEAD_DIM
    nblk = km_ref.shape[1]
    blk_rows = lax.broadcasted_iota(jnp.int32, (nblk, tq), 0)
    lane_head = lax.broadcasted_iota(jnp.int32, (1, GROUP_WIDTH), 1) // HEAD_DIM
    km = km_ref[0]
    gates_t = _mm(jnp.concatenate([jnp.where(lane_head == h, km, 0.0) for h in range(nh)], axis=0), qt,
                  precision=HIGHEST)
    _flash_init(m_ref, acc_ref)
    kd = _k_tile(k_ref, i, tq)
    for h in range(nh):
        gate_t = jnp.where(blk_rows < i, gates_t[h * nblk:(h + 1) * nblk], NEG_INF)
        sel_ref[h] = _topk_bias_t(gate_t, MOBA_TOPK)
        rhs_ref[h] = (jnp.where(row_head == h, qt, 0.0) * scale).astype(BF16)
        tab_ref[h] = rel * (-slopes[h] * LOG2E)
    mx = [_score(s_ref, h, jnp.where(causal, _mm(kd, rhs_ref[h]) + tab_ref[h], NEG_INF)) for h in range(nh)]
    for h in range(nh):
        _flash_update(s_ref, h, mx[h], 0.0, _with_ones(vt_ref[i, _head_rows(h), :]), h, m_ref, acc_ref)

    kn = jnp.max(kn_ref[...], axis=0)
    nback = [_reach_tiles(rhs_ref[h], m_ref[h:h + 1, :], kn[0:1, h:h + 1], slopes[h] * LOG2E, i, tq)
             for h in range(nh)]

    def make_stage(heads):
        nc = len(heads)

        def score(j, slot):
            kj = _k_tile(k_ref, j, tq)
            return tuple(_score(s_ref, slot * nc + n, _mm(kj, rhs_ref[h]) + tab_ref[h])
                         for n, h in enumerate(heads))

        def absorb(j, slot, mx):
            dj = (i - j).astype(F32)
            for n, h in enumerate(heads):
                cj = dj * (-slopes[h] * LOG2E * tq) + sel_ref[h, pl.ds(j, 1), :]
                _flash_update(s_ref, slot * nc + n, mx[n], cj, _with_ones(vt_ref[j, _head_rows(h), :]), h,
                              m_ref, acc_ref)
        return score, absorb, nc

    _segmented_sweep(i, nback, make_stage, s_ref.shape[0])
    for h in range(nh):
        ot_ref[_head_rows(h), :] = _normalized(acc_ref, h)
    o_ref[...] = ot_ref[...].T.astype(o_ref.dtype)


def _moba(qt, k, vt, kmean, knorm, b, s):
    nq = s // TQ
    return pl.pallas_call(
        functools.partial(_moba_kernel, slopes=_alibi_slopes(0), tq=TQ),
        grid=(b, nq),
        in_specs=[_qt_spec(GROUP_WIDTH, nq), _k_spec(s, GROUP_WIDTH), _vt_spec(nq, GROUP_WIDTH),
                  pl.BlockSpec((1, MOBA_ROWS, GROUP_WIDTH), lambda bb, i: (bb, 0, 0)), _kn_spec(nq)],
        out_specs=_row_spec(GROUP_WIDTH, nq),
        out_shape=jax.ShapeDtypeStruct((b * s, GROUP_WIDTH), BF16),
        scratch_shapes=[pltpu.VMEM((GROUP_HEADS, GROUP_WIDTH, TQ), BF16),
                        pltpu.VMEM((GROUP_HEADS, TQ, TQ), F32),
                        pltpu.VMEM((GROUP_HEADS, MOBA_ROWS, TQ), F32)]
        + _flash_scratch(GROUP_HEADS, SWEEP_GROUP * GROUP_HEADS),
        compiler_params=_params("parallel", "arbitrary"),
        name="moba",
    )(qt, k.reshape(b, s, GROUP_WIDTH), vt, kmean, knorm)


def _nsa_cmp_kernel(x_ref, pa_ref, pb_ref, w1a_ref, w1b_ref, b1_ref, w2_ref, o_ref, *, transposed):
    x = x_ref[0]
    n = x.shape[0]
    a = _mm(x + pa_ref[...], w1a_ref[...], precision=HIGHEST)
    bm = _mm(x + pb_ref[...], w1b_ref[...], precision=HIGHEST)
    hid = jax.nn.gelu(a + pltpu.roll(bm, n - 1, 0) + b1_ref[...], approximate=True)
    if transposed:
        o_ref[0] = _nt(w2_ref[...], hid, precision=HIGHEST)
    else:
        o_ref[0] = _mm(hid, w2_ref[...], precision=HIGHEST)


def _nsa_compress(x, pos, w1, b1, w2, b, s, transposed):
    rows = s // NSA_CMP_STRIDE
    half = NSA_CMP_STRIDE * HEAD_DIM
    hid = w1.shape[-1]
    full = lambda shape: pl.BlockSpec(shape, lambda bb: (0,) * len(shape))
    if transposed:
        w2 = w2.T
        oshape = (b, HEAD_DIM, rows)
    else:
        w2 = jnp.pad(w2, ((0, 0), (0, 128 - HEAD_DIM)))
        oshape = (b, rows, 128)
    return pl.pallas_call(
        functools.partial(_nsa_cmp_kernel, transposed=transposed),
        grid=(b,),
        in_specs=[pl.BlockSpec((1, rows, half), lambda bb: (bb, 0, 0)),
                  full((1, half)), full((1, half)), full((half, hid)), full((half, hid)),
                  full((1, hid)), full(w2.shape)],
        out_specs=pl.BlockSpec((1,) + oshape[1:], lambda bb: (bb, 0, 0)),
        out_shape=jax.ShapeDtypeStruct(oshape, F32),
        compiler_params=_params("parallel"),
        name="nsa_compress",
    )(x.reshape(b, rows, half),
      pos[:NSA_CMP_STRIDE].reshape(1, half), pos[NSA_CMP_STRIDE:].reshape(1, half),
      w1[:NSA_CMP_STRIDE].reshape(half, hid), w1[NSA_CMP_STRIDE:].reshape(half, hid),
      b1.reshape(1, hid), w2)


def _nsa_sel_kernel(qt_ref, kc_ref, vct_ref, ov_ref, oc_ref, sel_ref, sc_ref, ps_ref, *, slopes, tq):
    i = pl.program_id(1)
    scale = HEAD_DIM ** -0.5
    nh = GROUP_HEADS
    nc = kc_ref.shape[1]
    ck = 64
    qt = qt_ref[0].astype(F32)
    kc = kc_ref[0].astype(BF16)
    vct = vct_ref[0].astype(BF16)
    zeros = jnp.zeros((HEAD_DIM, tq), BF16)
    t_row = i * tq + lax.broadcasted_iota(jnp.int32, (1, tq), 1)

    def visible(r, rows):
        cmp_end = (r + lax.broadcasted_iota(jnp.int32, (rows, tq), 0)) * NSA_CMP_STRIDE + (NSA_CMP_LEN - 1)
        return cmp_end, cmp_end <= t_row

    mx = []
    for h in range(nh):
        qs = (qt[_head_rows(h), :] * scale).astype(BF16)
        cmp_end, mask = visible(0, nc)
        sc = _mm(kc, jnp.concatenate([qs, zeros], axis=0)) - slopes[h] * (t_row - cmp_end).astype(F32)
        mx.append(jnp.max(_score(sc_ref, h, jnp.where(mask, sc, NEG_INF)), axis=0, keepdims=True))
    den = []
    for h in range(nh):
        l8 = jnp.zeros((8, tq), F32)
        for r in range(0, nc, ck):
            _, mask = visible(r, ck)
            p = jnp.where(mask, jnp.exp(sc_ref[h, r:r + ck, :] - mx[h]), 0.0)
            sc_ref[h, r:r + ck, :] = p
            l8 = l8 + jnp.sum(p.reshape(ck // 8, 8, tq), axis=0)
        den.append(jnp.maximum(jnp.sum(l8, axis=0, keepdims=True), TINY))
    pieces = [[] for _ in range(nh)]
    for r in range(0, nc, ck):
        tot = None
        for h in range(nh):
            pn = sc_ref[h, r:r + ck, :] / den[h]
            pieces[h].append(pn.astype(BF16))
            tot = pn if tot is None else tot + pn
        ps_ref[r:r + ck, :] = tot
    for h in range(nh):
        oc_ref[0, _head_rows(h), :] = _mm(vct, jnp.concatenate(pieces[h], axis=0))
    imp_t = _mm(ov_ref[...], ps_ref[...], precision=HIGHEST)
    blk = lax.broadcasted_iota(jnp.int32, (SEL_ROWS, tq), 0)
    tt = i * tq + lax.broadcasted_iota(jnp.int32, (SEL_ROWS, tq), 1)
    qblk = lax.shift_right_logical(tt, 6)
    forced = (blk == 0) | (blk == qblk) | (blk == qblk - 1)
    score = jnp.where(forced, FORCED_SCORE, imp_t)
    score = jnp.where(blk * NSA_SEL_BLOCK <= tt, score, NEG_INF)
    sel_ref[0] = _topk_bias_t(score, NSA_SEL_TOPN).astype(sel_ref.dtype)


def _nsa_select(qt, kc, vct, b, s):
    nq = s // TQ
    nc = s // NSA_CMP_STRIDE
    nsel = s // NSA_SEL_BLOCK
    assert nsel <= SEL_ROWS and NSA_SEL_TOPN <= nsel
    cs = np.arange(nc)[None, :] * NSA_CMP_STRIDE
    bs = np.arange(SEL_ROWS)[:, None] * NSA_SEL_BLOCK
    ov = np.clip(np.minimum(cs + NSA_CMP_LEN, bs + NSA_SEL_BLOCK) - np.maximum(cs, bs), 0, None) / NSA_CMP_LEN
    ov[:, nc - 1] = 0.0
    ov[nsel:] = 0.0
    return pl.pallas_call(
        functools.partial(_nsa_sel_kernel, slopes=_alibi_slopes(1), tq=TQ),
        grid=(b, nq),
        in_specs=[_qt_spec(GROUP_WIDTH, nq),
                  pl.BlockSpec((1, nc, 128), lambda bb, i: (bb, 0, 0)),
                  pl.BlockSpec((1, HEAD_DIM, nc), lambda bb, i: (bb, 0, 0)),
                  pl.BlockSpec((SEL_ROWS, nc), lambda bb, i: (0, 0))],
        out_specs=[_qt_spec(GROUP_WIDTH, nq), _qt_spec(SEL_ROWS, nq)],
        out_shape=[jax.ShapeDtypeStruct((b * nq, GROUP_WIDTH, TQ), F32),
                   jax.ShapeDtypeStruct((b * nq, SEL_ROWS, TQ), BF16)],
        scratch_shapes=[pltpu.VMEM((GROUP_HEADS, nc, TQ), F32), pltpu.VMEM((nc, TQ), F32)],
        compiler_params=_params("parallel", "arbitrary"),
        name="nsa_select",
    )(qt, kc, vct, jnp.asarray(ov.astype(np.float32)))


def _nsa_attn_kernel(qt_ref, ks_ref, vst_ref, kw_ref, vwt_ref, sel_ref, oc_ref, g_ref, kn_ref, o_ref,
                     rhs_ref, rhsw_ref, tab_ref, m_ref, acc_ref, s_ref, ot_ref, *, slopes, tq):
    i = pl.program_id(1)
    scale = HEAD_DIM ** -0.5 * LOG2E
    nh = GROUP_HEADS
    qt = qt_ref[0].astype(F32)
    key, qry = _rel_pos(tq)
    rel = (qry - key).astype(F32)
    causal = key <= qry
    frow = lax.broadcasted_iota(jnp.int32, (HEAD_DIM, tq), 0)
    qoff = lax.broadcasted_iota(jnp.int32, (1, tq), 1).astype(F32)
    zeros = jnp.zeros((HEAD_DIM, tq), BF16)
    _flash_init(m_ref, acc_ref)
    ksd = _k_tile(ks_ref, i, tq)
    kwd = _k_tile(kw_ref, i, tq)
    for h in range(nh):
        qs = (qt[_head_rows(h), :] * scale).astype(BF16)
        slope2 = float(np.float32(slopes[h] * LOG2E))
        s1, s2, s3 = _bf16_pieces(slope2)
        t = qoff * (-slope2)
        t1 = t.astype(BF16).astype(F32)
        t2 = (t - t1).astype(BF16).astype(F32)
        t3 = t - t1 - t2
        feat = jnp.where(frow == 0, s1, jnp.where(frow == 1, s2, jnp.where(frow == 2, s3, jnp.where(
            frow == 3, t1, jnp.where(frow == 4, t2, jnp.where(frow == 5, t3, 0.0))))))
        rhs_ref[h, 0:AUG_BLK0, :] = qs
        rhs_ref[h, AUG_BLK0:AUG_POS0, :] = sel_ref[0]
        rhs_ref[h, AUG_POS0:GROUP_WIDTH, :] = feat.astype(BF16)
        rhsw_ref[h] = jnp.concatenate([qs, zeros], axis=0)
        tab_ref[h] = rel * (-slope2)
    vsd = _with_ones(vst_ref[i])
    vwd = _with_ones(vwt_ref[i])
    mx = [_score(s_ref, h, jnp.where(causal, _mm(ksd, rhs_ref[h]), NEG_INF)) for h in range(nh)]
    for h in range(nh):
        _flash_update(s_ref, h, mx[h], 0.0, vsd, h, m_ref, acc_ref)
    mx = [_score(s_ref, h, jnp.where(causal, _mm(kwd, rhsw_ref[h]) + tab_ref[h], NEG_INF)) for h in range(nh)]
    for h in range(nh):
        _flash_update(s_ref, h, mx[h], 0.0, vwd, nh + h, m_ref, acc_ref)

    kn = jnp.max(kn_ref[...], axis=0)
    nback = [_reach_tiles(rhs_ref[h, 0:AUG_BLK0, :], m_ref[h:h + 1, :], kn[1:2, 0:1], slopes[h] * LOG2E, i, tq)
             for h in range(nh)]

    def make_stage(heads):
        nc = len(heads)

        def score(j, slot):
            ksj = _k_tile(ks_ref, j, tq)
            return tuple(_score(s_ref, slot * nc + n, _mm(ksj, rhs_ref[h])) for n, h in enumerate(heads))

        def absorb(j, slot, mx):
            vsj = _with_ones(vst_ref[j])
            dj = (i - j).astype(F32)
            for n, h in enumerate(heads):
                _flash_update(s_ref, slot * nc + n, mx[n], dj * (-slopes[h] * LOG2E * tq), vsj, h, m_ref, acc_ref)
        return score, absorb, nc

    _segmented_sweep(i, nback, make_stage, s_ref.shape[0])

    for back in range(1, NSA_WINDOW // tq + 1):
        @pl.when(i >= back)
        def _(back=back):
            kwj = _k_tile(kw_ref, i - back, tq)
            vwj = _with_ones(vwt_ref[i - back])
            mx = []
            for h in range(nh):
                sv = _mm(kwj, rhsw_ref[h]) + tab_ref[h]
                if back * tq == NSA_WINDOW:
                    sv = jnp.where(key > qry, sv, NEG_INF)
                mx.append(_score(s_ref, h, sv))
            for h in range(nh):
                _flash_update(s_ref, h, mx[h], -slopes[h] * LOG2E * tq * back, vwj, nh + h, m_ref, acc_ref)

    for h in range(nh):
        rows = _head_rows(h)
        ot_ref[rows, :] = (g_ref[0, rows, :] * oc_ref[0, rows, :]
                           + g_ref[0, _head_rows(nh + h), :] * _normalized(acc_ref, h)
                           + g_ref[0, _head_rows(2 * nh + h), :] * _normalized(acc_ref, nh + h))
    o_ref[...] = ot_ref[...].T.astype(o_ref.dtype)


def _nsa_attend(qt, ks, vst, kw, vwt, sel, oct, gt, knorm, b, s):
    nq = s // TQ
    return pl.pallas_call(
        functools.partial(_nsa_attn_kernel, slopes=_alibi_slopes(1), tq=TQ),
        grid=(b, nq),
        in_specs=[_qt_spec(GROUP_WIDTH, nq), _k_spec(s, GROUP_WIDTH), _vt_spec(nq, HEAD_DIM),
                  _k_spec(s, 128), _vt_spec(nq, HEAD_DIM), _qt_spec(SEL_ROWS, nq), _qt_spec(GROUP_WIDTH, nq),
                  _qt_spec(3 * GROUP_WIDTH, nq), _kn_spec(nq)],
        out_specs=_row_spec(GROUP_WIDTH, nq),
        out_shape=jax.ShapeDtypeStruct((b * s, GROUP_WIDTH), BF16),
        scratch_shapes=[pltpu.VMEM((GROUP_HEADS, GROUP_WIDTH, TQ), BF16),
                        pltpu.VMEM((GROUP_HEADS, 128, TQ), BF16),
                        pltpu.VMEM((GROUP_HEADS, TQ, TQ), F32)]
        + _flash_scratch(2 * GROUP_HEADS, SWEEP_GROUP * GROUP_HEADS),
        compiler_params=_params("parallel", "arbitrary"),
        name="nsa_attend",
    )(qt, ks.reshape(b, s, GROUP_WIDTH), vst, kw.reshape(b, s, 128), vwt, sel, oct, gt, knorm)


def _diff_kernel(qt_ref, k_ref, vt_ref, lam_ref, sub_ref, kn_ref, o_ref, rhs_ref, tab_ref, m_ref, acc_ref, s_ref, ot_ref,
                 *, slopes, tq, lambda_init):
    i = pl.program_id(1)
    scale = DIFF_QK_DIM ** -0.5 * LOG2E
    nh = GROUP_HEADS
    qt = qt_ref[0].astype(F32)
    key, qry = _rel_pos(tq)
    rel = (qry - key).astype(F32)
    causal = key <= qry
    row = lax.broadcasted_iota(jnp.int32, (GROUP_WIDTH, 1), 0)
    _flash_init(m_ref, acc_ref)
    kd = _k_tile(k_ref, i, tq)
    for h in range(nh):
        lo = h * HEAD_DIM
        rhs_ref[h] = (jnp.where((row >= lo) & (row < lo + DIFF_QK_DIM), qt, 0.0) * scale).astype(BF16)
        rhs_ref[nh + h] = (jnp.where((row >= lo + DIFF_QK_DIM) & (row < lo + HEAD_DIM), qt, 0.0)
                           * scale).astype(BF16)
        tab_ref[h] = rel * (-slopes[h] * LOG2E)
    mx = [_score(s_ref, c, jnp.where(causal, _mm(kd, rhs_ref[c]) + tab_ref[c % nh], NEG_INF))
          for c in range(2 * nh)]
    for c in range(2 * nh):
        _flash_update(s_ref, c, mx[c], 0.0, _with_ones(vt_ref[i, _head_rows(c % nh), :]), c, m_ref, acc_ref)

    kn = jnp.max(kn_ref[...], axis=0)
    nback = [jnp.maximum(
        _reach_tiles(rhs_ref[h], m_ref[h:h + 1, :], kn[2:3, 2 * h:2 * h + 1], slopes[h] * LOG2E, i, tq),
        _reach_tiles(rhs_ref[nh + h], m_ref[nh + h:nh + h + 1, :], kn[2:3, 2 * h + 1:2 * h + 2],
                     slopes[h] * LOG2E, i, tq)) for h in range(nh)]

    def make_stage(heads):
        chains = tuple(heads) + tuple(nh + h for h in heads)
        nc = len(chains)

        def score(j, slot):
            kj = _k_tile(k_ref, j, tq)
            return tuple(_score(s_ref, slot * nc + n, _mm(kj, rhs_ref[c]) + tab_ref[c % nh])
                         for n, c in enumerate(chains))

        def absorb(j, slot, mx):
            dj = (i - j).astype(F32)
            for n, c in enumerate(chains):
                h = c % nh
                _flash_update(s_ref, slot * nc + n, mx[n], dj * (-slopes[h] * LOG2E * tq),
                              _with_ones(vt_ref[j, _head_rows(h), :]), c, m_ref, acc_ref)
        return score, absorb, nc

    _segmented_sweep(i, nback, make_stage, s_ref.shape[0])
    lam_p = lam_ref[...]
    lam = (jnp.exp(jnp.sum(lam_p[0:1] * lam_p[1:2], axis=-1, keepdims=True))
           - jnp.exp(jnp.sum(lam_p[2:3] * lam_p[3:4], axis=-1, keepdims=True)) + lambda_init)
    for h in range(nh):
        o = _normalized(acc_ref, h) - lam * _normalized(acc_ref, nh + h)
        ms = jnp.mean(o * o, axis=0, keepdims=True)
        ot_ref[_head_rows(h), :] = ((o * lax.rsqrt(ms + NORM_EPS)) * sub_ref[...]) * (1.0 - lambda_init)
    o_ref[...] = ot_ref[...].T.astype(o_ref.dtype)


def _diff(qt, k, vt, lam, subln, lambda_init, knorm, b, s):
    nq = s // TQ
    return pl.pallas_call(
        functools.partial(_diff_kernel, slopes=_alibi_slopes(2), tq=TQ, lambda_init=lambda_init),
        grid=(b, nq),
        in_specs=[_qt_spec(GROUP_WIDTH, nq), _k_spec(s, GROUP_WIDTH), _vt_spec(nq, GROUP_WIDTH),
                  pl.BlockSpec((4, DIFF_QK_DIM), lambda bb, i: (0, 0)),
                  pl.BlockSpec((HEAD_DIM, TQ), lambda bb, i: (0, 0)), _kn_spec(nq)],
        out_specs=_row_spec(GROUP_WIDTH, nq),
        out_shape=jax.ShapeDtypeStruct((b * s, GROUP_WIDTH), BF16),
        scratch_shapes=[pltpu.VMEM((2 * GROUP_HEADS, GROUP_WIDTH, TQ), BF16),
                        pltpu.VMEM((GROUP_HEADS, TQ, TQ), F32)]
        + _flash_scratch(2 * GROUP_HEADS, SWEEP_GROUP * 2 * GROUP_HEADS),
        compiler_params=_params("parallel", "arbitrary"),
        name="diff_attn",
    )(qt, k.reshape(b, s, GROUP_WIDTH), vt, lam, jnp.broadcast_to(subln[:, None], (HEAD_DIM, TQ)), knorm)


def _swa_kernel(qt_ref, k_ref, vt_ref, sink_ref, o_ref, rhs_ref, tab_ref, m_ref, acc_ref, s_ref, ot_ref,
                *, slopes, tq):
    i = pl.program_id(1)
    scale = HEAD_DIM ** -0.5 * LOG2E
    nh = GROUP_HEADS
    per_kv = nh // SWA_KV_HEADS
    qt = qt_ref[0].astype(F32)
    key, qry = _rel_pos(tq)
    d = qry - key
    rel = d.astype(F32)
    zeros = jnp.zeros((HEAD_DIM, tq), BF16)
    _flash_init(m_ref, acc_ref)
    kd = _k_tile(k_ref, i, tq)
    for h in range(nh):
        qs = (qt[_head_rows(h), :] * scale).astype(BF16)
        rhs_ref[h] = jnp.concatenate([qs, zeros] if h // per_kv == 0 else [zeros, qs], axis=0)
        tab_ref[h] = rel * (-slopes[h] * LOG2E)
    in_window = (d >= 0) & (d < SWA_WINDOW)
    mx = [_score(s_ref, h, jnp.where(in_window, _mm(kd, rhs_ref[h]) + tab_ref[h], NEG_INF)) for h in range(nh)]
    for h in range(nh):
        _flash_update(s_ref, h, mx[h], 0.0, _with_ones(vt_ref[i, _head_rows(h // per_kv), :]), h, m_ref, acc_ref)

    @pl.when(i >= 1)
    def _():
        kj = _k_tile(k_ref, i - 1, tq)
        mx = [_score(s_ref, h, jnp.where(d + tq < SWA_WINDOW, _mm(kj, rhs_ref[h]) + tab_ref[h], NEG_INF))
              for h in range(nh)]
        for h in range(nh):
            _flash_update(s_ref, h, mx[h], -slopes[h] * LOG2E * tq,
                          _with_ones(vt_ref[i - 1, _head_rows(h // per_kv), :]), h, m_ref, acc_ref)

    for h in range(nh):
        sink = sink_ref[h:h + 1, 0:1] * LOG2E
        m_old = m_ref[h:h + 1, :]
        m_new = jnp.maximum(m_old, sink)
        alpha = jnp.exp2(m_old - m_new)
        a = acc_ref[h]
        denom = alpha * a[HEAD_DIM:HEAD_DIM + 1] + jnp.exp2(sink - m_new)
        ot_ref[_head_rows(h), :] = (alpha * a[0:HEAD_DIM]) / denom
    o_ref[...] = ot_ref[...].T.astype(o_ref.dtype)


def _swa(qt, k, vt, sinks, b, s):
    nq = s // TQ
    kvw = SWA_KV_HEADS * HEAD_DIM
    return pl.pallas_call(
        functools.partial(_swa_kernel, slopes=_alibi_slopes(3), tq=TQ),
        grid=(b, nq),
        in_specs=[_qt_spec(GROUP_WIDTH, nq), _k_spec(s, kvw), _vt_spec(nq, kvw),
                  pl.BlockSpec((8, 128), lambda bb, i: (0, 0))],
        out_specs=_row_spec(GROUP_WIDTH, nq),
        out_shape=jax.ShapeDtypeStruct((b * s, GROUP_WIDTH), BF16),
        scratch_shapes=[pltpu.VMEM((GROUP_HEADS, kvw, TQ), BF16),
                        pltpu.VMEM((GROUP_HEADS, TQ, TQ), F32)] + _flash_scratch(GROUP_HEADS, GROUP_HEADS),
        compiler_params=_params("parallel", "arbitrary"),
        name="swa",
    )(qt, k.reshape(b, s, kvw), vt, jnp.pad(jnp.broadcast_to(sinks[:, None], (GROUP_HEADS, 128)), ((0, 4), (0, 0))))


def _out_proj_kernel(x_ref, a_ref, b_ref, c_ref, d_ref, w_ref, g_ref, o_ref):
    y = None
    for g, m_ref in enumerate((a_ref, b_ref, c_ref, d_ref)):
        part = _mm(m_ref[...], w_ref[g * GROUP_WIDTH:(g + 1) * GROUP_WIDTH, :])
        y = part if y is None else y + part
    o_ref[...] = x_ref[...] + _rms(y, g_ref[...])


def _out_proj(x2, mixes, w, gain):
    n, d = x2.shape
    tm = 512
    row = lambda wd: pl.BlockSpec((tm, wd), lambda i: (i, 0))
    return pl.pallas_call(
        _out_proj_kernel,
        grid=(n // tm,),
        in_specs=[row(d)] + [row(GROUP_WIDTH)] * 4 + [pl.BlockSpec((d, d), lambda i: (0, 0)),
                                                     pl.BlockSpec((1, d), lambda i: (0, 0))],
        out_specs=row(d),
        out_shape=jax.ShapeDtypeStruct((n, d), F32),
        compiler_params=_params("parallel"),
        name="out_proj",
    )(x2, *mixes, w, gain.reshape(1, d))


def _ffn_kernel(x_ref, xp_ref, gpre_ref, wg_ref, wu_ref, cw_ref, cb_ref, wd_ref, gpost_ref, o_ref,
                h_ref, hp_ref, acc_ref, *, tiles_per_seq):
    i = pl.program_id(0)
    f = pl.program_id(1)

    @pl.when(f == 0)
    def _():
        h_ref[...] = _rms(x_ref[...], gpre_ref[...]).astype(BF16)
        hp = _rms(xp_ref[...], gpre_ref[...])
        hp_ref[...] = jnp.where(i % tiles_per_seq != 0, hp, 0.0).astype(BF16)
        acc_ref[...] = jnp.zeros(acc_ref.shape, F32)

    h = h_ref[...]
    a = _mm(h, wg_ref[...])
    ap = _mm(hp_ref[...], wg_ref[...])
    row = lax.broadcasted_iota(jnp.int32, a.shape, 0)
    a1 = jnp.where(row == 0, ap[7:8], pltpu.roll(a, 1, 0))
    a2 = jnp.where(row == 0, ap[6:7], jnp.where(row == 1, ap[7:8], pltpu.roll(a, 2, 0)))
    cw = cw_ref[...]
    conv = cw[0:1] * a2 + cw[1:2] * a1 + cw[2:3] * a + cb_ref[...]
    gated = jax.nn.gelu(conv, approximate=True) * _mm(h, wu_ref[...])
    acc_ref[...] += _mm(gated.astype(BF16), wd_ref[...])

    @pl.when(f == pl.num_programs(1) - 1)
    def _():
        o_ref[...] = x_ref[...] + _rms(acc_ref[...], gpost_ref[...])


def _ffn(x2, gpre, wg, wu, cw, cb, wd, gpost, s):
    n, d = x2.shape
    dff = wg.shape[1]
    tm, tf = 1024, 512
    halo = 8
    return pl.pallas_call(
        functools.partial(_ffn_kernel, tiles_per_seq=s // tm),
        grid=(n // tm, dff // tf),
        in_specs=[pl.BlockSpec((tm, d), lambda i, f: (i, 0)),
                  pl.BlockSpec((halo, d), lambda i, f: (jnp.maximum(i * (tm // halo) - 1, 0), 0)),
                  pl.BlockSpec((1, d), lambda i, f: (0, 0)),
                  pl.BlockSpec((d, tf), lambda i, f: (0, f)),
                  pl.BlockSpec((d, tf), lambda i, f: (0, f)),
                  pl.BlockSpec((CONV_WIDTH, tf), lambda i, f: (0, f)),
                  pl.BlockSpec((1, tf), lambda i, f: (0, f)),
                  pl.BlockSpec((tf, d), lambda i, f: (f, 0)),
                  pl.BlockSpec((1, d), lambda i, f: (0, 0))],
        out_specs=pl.BlockSpec((tm, d), lambda i, f: (i, 0)),
        out_shape=jax.ShapeDtypeStruct((n, d), F32),
        scratch_shapes=[pltpu.VMEM((tm, d), BF16), pltpu.VMEM((halo, d), BF16), pltpu.VMEM((tm, d), F32)],
        compiler_params=_params("parallel", "arbitrary"),
        name="conv_ffn",
    )(x2, x2, gpre.reshape(1, d), wg, wu, cw, cb.reshape(1, dff), wd, gpost.reshape(1, d))


def kernel(x, attn_pre_norm, attn_post_norm, ffn_pre_norm, ffn_post_norm, w_in, w_out, nsa_cmp_pos_k, nsa_cmp_w1_k, nsa_cmp_b1_k, nsa_cmp_w2_k, nsa_cmp_pos_v, nsa_cmp_w1_v, nsa_cmp_b1_v, nsa_cmp_w2_v, diff_lambda_q1, diff_lambda_k1, diff_lambda_q2, diff_lambda_k2, diff_subln, swa_sinks, ffn_w_gate, ffn_w_up, ffn_conv_w, ffn_conv_b, ffn_w_down):
    b, s, d = x.shape
    depth = w_in.shape[0]
    assert s % 1024 == 0 and s // MOBA_BLOCK <= MOBA_ROWS
    ncols, tcols = (jnp.asarray(c) for c in _in_columns())
    x2 = x.reshape(b * s, d)
    for l in range(depth):
        w_ext = jnp.pad(w_in[l], ((0, 0), (0, 1)))
        p = _in_proj(x2, attn_pre_norm[l], jnp.take(w_ext, ncols, axis=1).astype(BF16),
                     jnp.take(w_ext, tcols, axis=1).T.astype(BF16), s)
        kmean = p["kmean"].reshape(b, s // MOBA_BLOCK, GROUP_WIDTH)
        kmean = jnp.pad(kmean, ((0, 0), (0, MOBA_ROWS - s // MOBA_BLOCK), (0, 0)))
        o_moba = _moba(p["mq"], p["mk"], p["mv"], kmean, p["knorm"], b, s)
        kc = _nsa_compress(p["ncmp"][:, :HEAD_DIM], nsa_cmp_pos_k[l], nsa_cmp_w1_k[l], nsa_cmp_b1_k[l],
                           nsa_cmp_w2_k[l], b, s, False)
        vct = _nsa_compress(p["ncmp"][:, HEAD_DIM:], nsa_cmp_pos_v[l], nsa_cmp_w1_v[l], nsa_cmp_b1_v[l],
                            nsa_cmp_w2_v[l], b, s, True)
        o_cmp, sel = _nsa_select(p["nq"], kc, vct, b, s)
        o_nsa = _nsa_attend(p["nq"], p["nks"], p["nvs"], p["nkw"], p["nvw"], sel, o_cmp, p["gates"],
                            p["knorm"], b, s)
        lam = jnp.stack([diff_lambda_q1[l], diff_lambda_k1[l], diff_lambda_q2[l], diff_lambda_k2[l]])
        lambda_init = 0.8 - 0.6 * math.exp(-0.3 * l)
        o_diff = _diff(p["dq"], p["dk"], p["dv"], lam, diff_subln[l], lambda_init, p["knorm"], b, s)
        o_swa = _swa(p["sq"], p["sk"], p["sv"], swa_sinks[l], b, s)
        x2 = _out_proj(x2, (o_moba, o_nsa, o_diff, o_swa), w_out[l].astype(BF16), attn_post_norm[l])
        x2 = _ffn(x2, ffn_pre_norm[l], ffn_w_gate[l].astype(BF16), ffn_w_up[l].astype(BF16), ffn_conv_w[l],
                  ffn_conv_b[l], ffn_w_down[l].astype(BF16), ffn_post_norm[l], s)
    return x2.reshape(b, s, d)
```

```python
import functools
import math

import numpy as np
import jax
import jax.numpy as jnp
from jax import lax
from jax.experimental import pallas as pl
from jax.experimental.pallas import tpu as pltpu

F32 = jnp.float32
BF16 = jnp.bfloat16
HIGHEST = lax.Precision.HIGHEST

N_MIXERS = 4
N_HEADS = 16
GROUP_HEADS = 4
HEAD_DIM = 64
GROUP_WIDTH = 256
MOBA_BLOCK = 256
MOBA_TOPK = 3
NSA_CMP_LEN = 32
NSA_CMP_STRIDE = 16
NSA_SEL_BLOCK = 64
NSA_SEL_TOPN = 16
NSA_WINDOW = 512
DIFF_QK_DIM = 32
SWA_KV_HEADS = 2
SWA_WINDOW = 128
CONV_WIDTH = 3
NORM_EPS = 1e-6
NEG_INF = -1e30
TINY = 1e-30
FORCED_SCORE = 1e4
LOG2E = 1.4426950408889634
ONES_ROWS = 16
UNDERFLOW_MARGIN = 160.0
REMOVED = -3e38

TQ = 256
SWEEP_GROUP = 4
SEL_ROWS = 128
MOBA_ROWS = 32
VMEM_LIMIT = 56 * 1024 * 1024

AUG_BLK0 = HEAD_DIM
AUG_POS0 = AUG_BLK0 + SEL_ROWS
AUG_ONE0 = AUG_POS0 + 3


def _alibi_slopes(mixer):
    slopes = np.power(np.float32(2.0), np.arange(1, N_HEADS + 1, dtype=np.float32) * np.float32(-8.0 / N_HEADS))
    return [float(s) for s in slopes[mixer::N_MIXERS]]


def _bf16_pieces(x):
    out, rem = [], np.float32(x)
    for _ in range(3):
        p = np.float32(np.asarray(rem, np.float32).astype(jnp.bfloat16).astype(np.float32))
        out.append(float(p))
        rem = np.float32(rem - p)
    assert rem == 0.0
    return out


def _nt(a, b, precision=None):
    return lax.dot_general(a, b, (((1,), (1,)), ((), ())), preferred_element_type=F32, precision=precision)


def _mm(a, b, precision=None):
    return jnp.dot(a, b, preferred_element_type=F32, precision=precision)


def _rms(x, gain):
    return (x * lax.rsqrt(jnp.mean(x * x, axis=-1, keepdims=True) + NORM_EPS)) * gain


def _params(*sem):
    return pltpu.CompilerParams(dimension_semantics=sem, vmem_limit_bytes=VMEM_LIMIT)


_N_GROUPS = (("mk", 256, BF16), ("nks", 256, BF16), ("nkw", 128, BF16), ("dk", 256, BF16), ("sk", 128, BF16),
             ("ncmp", 128, F32))
_T_GROUPS = (("mq", 256, BF16), ("nq", 256, BF16), ("dq", 256, BF16), ("sq", 256, BF16),
             ("mv", 256, BF16), ("dv", 256, BF16), ("nvs", 64, BF16), ("nvw", 64, BF16), ("sv", 128, BF16),
             ("gates", 768, F32))
_NORM_ROWS = {"mk": (0, HEAD_DIM), "nks": (1, HEAD_DIM), "dk": (2, DIFF_QK_DIM)}
_ZERO_COL = 2700


def _in_columns():
    r = np.arange
    z = lambda n: np.full(n, _ZERO_COL)
    ncols = np.concatenate([r(256, 512), r(1152, 1216), z(192), r(1280, 1344), z(64), r(1676, 1932),
                            r(2444, 2572), r(1024, 1152)])
    gate = np.asarray([1408 + 3 * h + j for j in range(3) for h in range(4) for _ in range(64)])
    tcols = np.concatenate([r(0, 256), r(768, 1024), r(1420, 1676), r(2188, 2444), r(512, 768), r(1932, 2188),
                            r(1216, 1280), r(1344, 1408), r(2572, 2700), gate])
    assert ncols.shape[0] == sum(w for _, w, _ in _N_GROUPS) and tcols.shape[0] == sum(w for _, w, _ in _T_GROUPS)
    return ncols, tcols


def _in_proj_kernel(x_ref, g_ref, wn_ref, wt_ref, *out_refs, tm, seq):
    names = [n for n, _, _ in _N_GROUPS] + [n for n, _, _ in _T_GROUPS] + ["kmean", "knorm"]
    outs = dict(zip(names, out_refs))
    hb = _rms(x_ref[...], g_ref[...]).astype(BF16)
    off = 0
    for name, width, dtype in _N_GROUPS:
        res = _mm(hb, wn_ref[:, off:off + width])
        off += width
        if name == "mk":
            for r in range(tm // MOBA_BLOCK):
                blk = res[r * MOBA_BLOCK:(r + 1) * MOBA_BLOCK]
                outs["kmean"][0, r:r + 1, :] = jnp.mean(blk, axis=0, keepdims=True)
        if name in _NORM_ROWS:
            nrow, glanes = _NORM_ROWS[name]
            kb = res.astype(BF16).astype(F32)
            il = lax.broadcasted_iota(jnp.int32, (GROUP_WIDTH, 128), 0)
            ic = lax.broadcasted_iota(jnp.int32, (GROUP_WIDTH, 128), 1)
            nrm = _mm((kb * kb).astype(BF16), jnp.where(il // glanes == ic, 1.0, 0.0).astype(BF16))
            for c in range(tm // TQ):
                outs["knorm"][c, nrow:nrow + 1, :] = jnp.max(nrm[c * TQ:(c + 1) * TQ], axis=0, keepdims=True)
        if name == "nks":
            lane = lax.broadcasted_iota(jnp.int32, res.shape, 1)
            row = lax.broadcasted_iota(jnp.int32, res.shape, 0)
            pos = (pl.program_id(0) % (seq // tm)) * tm + row
            onehot = (lane - AUG_BLK0) == lax.shift_right_logical(pos, 6)
            feat = jnp.where((lane >= AUG_POS0) & (lane < AUG_ONE0), (row % TQ).astype(F32),
                             jnp.where((lane >= AUG_ONE0) & (lane < AUG_ONE0 + 3), 1.0, 0.0))
            res = jnp.where(lane < AUG_BLK0, res, jnp.where(lane < AUG_POS0, jnp.where(onehot, 1.0, 0.0), feat))
        outs[name][...] = res.astype(dtype)
    for c in range(tm // TQ):
        outs["knorm"][c, len(_NORM_ROWS):8, :] = jnp.zeros((8 - len(_NORM_ROWS), 128), F32)
    off = 0
    for name, width, dtype in _T_GROUPS:
        res = _nt(wt_ref[off:off + width, :], hb)
        off += width
        if name == "gates":
            res = 1.0 / (1.0 + jnp.exp(-res))
        for c in range(tm // TQ):
            outs[name][c] = res[:, c * TQ:(c + 1) * TQ].astype(dtype)


def _in_proj(x2, gain, wn, wt, seq):
    n, d = x2.shape
    tm = 512
    out_shape = [jax.ShapeDtypeStruct((n, wd), dt) for _, wd, dt in _N_GROUPS]
    out_specs = [pl.BlockSpec((tm, wd), lambda i: (i, 0)) for _, wd, _ in _N_GROUPS]
    out_shape += [jax.ShapeDtypeStruct((n // TQ, wd, TQ), dt) for _, wd, dt in _T_GROUPS]
    out_specs += [pl.BlockSpec((tm // TQ, wd, TQ), lambda i: (i, 0, 0)) for _, wd, _ in _T_GROUPS]
    out_shape.append(jax.ShapeDtypeStruct((n // tm, tm // MOBA_BLOCK, 256), F32))
    out_specs.append(pl.BlockSpec((1, tm // MOBA_BLOCK, 256), lambda i: (i, 0, 0)))
    out_shape.append(jax.ShapeDtypeStruct((n // TQ, 8, 128), F32))
    out_specs.append(pl.BlockSpec((tm // TQ, 8, 128), lambda i: (i, 0, 0)))
    outs = pl.pallas_call(
        functools.partial(_in_proj_kernel, tm=tm, seq=seq),
        grid=(n // tm,),
        in_specs=[pl.BlockSpec((tm, d), lambda i: (i, 0)),
                  pl.BlockSpec((1, d), lambda i: (0, 0)),
                  pl.BlockSpec(wn.shape, lambda i: (0, 0)),
                  pl.BlockSpec(wt.shape, lambda i: (0, 0))],
        out_specs=out_specs,
        out_shape=out_shape,
        compiler_params=_params("parallel"),
        name="in_proj",
    )(x2, gain.reshape(1, d), wn, wt)
    names = [nm for nm, _, _ in _N_GROUPS] + [nm for nm, _, _ in _T_GROUPS] + ["kmean", "knorm"]
    return dict(zip(names, outs))


def _rel_pos(tq):
    key = lax.broadcasted_iota(jnp.int32, (tq, tq), 0)
    qry = lax.broadcasted_iota(jnp.int32, (tq, tq), 1)
    return key, qry


def _head_rows(h, n=1):
    return slice(h * HEAD_DIM, (h + n) * HEAD_DIM)


def _flash_init(m_ref, acc_ref):
    m_ref[...] = jnp.full(m_ref.shape, NEG_INF, F32)
    acc_ref[...] = jnp.zeros(acc_ref.shape, F32)


def _with_ones(vt):
    return jnp.concatenate([vt, jnp.ones((ONES_ROWS, vt.shape[1]), BF16)], axis=0)


def _score(s_ref, c, sv):
    s_ref[c] = sv
    tk, tq = sv.shape
    return jnp.max(sv.reshape(tk // 8, 8, tq), axis=0)


def _flash_update(s_ref, c, mx8, cj, vt, idx, m_ref, acc_ref):
    tk = s_ref.shape[1]
    m_old = m_ref[idx:idx + 1, :]
    m_new = jnp.maximum(m_old, jnp.max(mx8, axis=0, keepdims=True) + cj)
    shift = m_new - cj
    ps = [jnp.exp2(s_ref[c, r * 64:(r + 1) * 64, :] - shift).astype(BF16) for r in range(tk // 64)]
    alpha = jnp.exp2(m_old - m_new)
    acc_ref[idx] = alpha * acc_ref[idx] + _mm(vt, jnp.concatenate(ps, axis=0))
    m_ref[idx:idx + 1, :] = m_new


def _reach_tiles(rhs, m_row, kn2, slope2, i, tq):
    rf = rhs.astype(F32)
    qn2 = jnp.max(jnp.sum(rf * rf, axis=0, keepdims=True), axis=1, keepdims=True)
    bound = jnp.sqrt(qn2 * kn2) * 1.02
    reach = (bound - jnp.min(m_row, axis=1, keepdims=True) + UNDERFLOW_MARGIN) * (1.0 / slope2)
    nb = jnp.floor((reach - 1.0) * (1.0 / tq)) + 1.0
    return jnp.minimum(jnp.maximum(nb, 0.0), i.astype(F32))


def _segmented_sweep(i, nback, make_stage, staged_tiles):
    nh = len(nback)
    starts, prev = [], i
    for h in range(nh):
        st = jnp.minimum(i - nback[h].astype(jnp.int32)[0, 0], prev)
        starts.append(st)
        prev = st
    bounds = starts[::-1] + [i]
    for n in range(1, nh + 1):
        lo, hi = bounds[n - 1], bounds[n]
        score, absorb, per_slot = make_stage(tuple(range(nh - n, nh)))

        def step(j0, size, score=score, absorb=absorb):
            mx = [score(j0 + t, t) for t in range(size)]
            for t in range(size):
                absorb(j0 + t, t, mx[t])

        gsz = SWEEP_GROUP * 2 if n == 1 else SWEEP_GROUP
        assert gsz * per_slot <= staged_tiles

        def group(g, carry, lo=lo, step=step, gsz=gsz):
            step(lo + gsz * g, gsz)
            return carry

        ngroups = (hi - lo) // gsz
        lax.fori_loop(0, ngroups, group, 0)
        rem, done = (hi - lo) % gsz, lo + gsz * ngroups
        size = gsz // 2
        while size >= 1:
            @pl.when((rem // size) % 2 == 1)
            def _(size=size, start=done + (rem // (2 * size)) * (2 * size), step=step):
                step(start, size)
            size //= 2


def _normalized(acc_ref, idx):
    a = acc_ref[idx]
    return a[0:HEAD_DIM] / a[HEAD_DIM:HEAD_DIM + 1]


def _topk_bias_t(score_t, k):
    nrow = score_t.shape[0]
    rows = lax.broadcasted_iota(jnp.int32, score_t.shape, 0).astype(F32)
    bias = jnp.full(score_t.shape, NEG_INF, F32)
    work = score_t
    for _ in range(k):
        m = jnp.max(work, axis=0, keepdims=True)
        idx = jnp.min(jnp.where(work == m, rows, float(nrow)), axis=0, keepdims=True)
        pick = rows == idx
        bias = jnp.where(pick, jnp.where(m > 0.5 * NEG_INF, 0.0, NEG_INF), bias)
        work = jnp.where(pick, REMOVED, work)
    return bias


def _k_tile(ref, j, tk):
    return ref[0, pl.ds(pl.multiple_of(j * tk, tk), tk), :]


def _qt_spec(width, nq):
    return pl.BlockSpec((1, width, TQ), lambda b, i: (b * nq + i, 0, 0))


def _kn_spec(nq):
    return pl.BlockSpec((nq, 8, 128), lambda b, i: (b, 0, 0))


def _k_spec(s, width):
    return pl.BlockSpec((1, s, width), lambda b, i: (b, 0, 0))


def _vt_spec(nq, width):
    return pl.BlockSpec((nq, width, TQ), lambda b, i: (b, 0, 0))


def _row_spec(width, nq):
    return pl.BlockSpec((TQ, width), lambda b, i: (b * nq + i, 0))


def _flash_scratch(chains, tiles):
    return [pltpu.VMEM((8, TQ), F32), pltpu.VMEM((chains, HEAD_DIM + ONES_ROWS, TQ), F32),
            pltpu.VMEM((tiles, TQ, TQ), F32), pltpu.VMEM((GROUP_WIDTH, TQ), F32)]


def _moba_kernel(qt_ref, k_ref, vt_ref, km_ref, kn_ref, o_ref, rhs_ref, tab_ref, sel_ref, m_ref, acc_ref, s_ref, ot_ref,
                 *, slopes, tq):
    i = pl.program_id(1)
    nh = GROUP_HEADS
    scale = HEAD_DIM ** -0.5 * LOG2E
    qt = qt_ref[0].astype(F32)
    key, qry = _rel_pos(tq)
    rel = (qry - key).astype(F32)
    causal = key <= qry
    row_head = lax.broadcasted_iota(jnp.int32, (GROUP_WIDTH, 1), 0) // HEAD_DIM
    nblk = km_ref.shape[1]
    blk_rows = lax.broadcasted_iota(jnp.int32, (nblk, tq), 0)
    lane_head = lax.broadcasted_iota(jnp.int32, (1, GROUP_WIDTH), 1) // HEAD_DIM
    km = km_ref[0]
    gates_t = _mm(jnp.concatenate([jnp.where(lane_head == h, km, 0.0) for h in range(nh)], axis=0), qt,
                  precision=HIGHEST)
    _flash_init(m_ref, acc_ref)
    kd = _k_tile(k_ref, i, tq)
    for h in range(nh):
        gate_t = jnp.where(blk_rows < i, gates_t[h * nblk:(h + 1) * nblk], NEG_INF)
        sel_ref[h] = _topk_bias_t(gate_t, MOBA_TOPK)
        rhs_ref[h] = (jnp.where(row_head == h, qt, 0.0) * scale).astype(BF16)
        tab_ref[h] = rel * (-slopes[h] * LOG2E)
    mx = [_score(s_ref, h, jnp.where(causal, _mm(kd, rhs_ref[h]) + tab_ref[h], NEG_INF)) for h in range(nh)]
    for h in range(nh):
        _flash_update(s_ref, h, mx[h], 0.0, _with_ones(vt_ref[i, _head_rows(h), :]), h, m_ref, acc_ref)

    kn = jnp.max(kn_ref[...], axis=0)
    nback = [_reach_tiles(rhs_ref[h], m_ref[h:h + 1, :], kn[0:1, h:h + 1], slopes[h] * LOG2E, i, tq)
             for h in range(nh)]

    def make_stage(heads):
        nc = len(heads)

        def score(j, slot):
            kj = _k_tile(k_ref, j, tq)
            return tuple(_score(s_ref, slot * nc + n, _mm(kj, rhs_ref[h]) + tab_ref[h])
                         for n, h in enumerate(heads))

        def absorb(j, slot, mx):
            dj = (i - j).astype(F32)
            for n, h in enumerate(heads):
                cj = dj * (-slopes[h] * LOG2E * tq) + sel_ref[h, pl.ds(j, 1), :]
                _flash_update(s_ref, slot * nc + n, mx[n], cj, _with_ones(vt_ref[j, _head_rows(h), :]), h,
                              m_ref, acc_ref)
        return score, absorb, nc

    _segmented_sweep(i, nback, make_stage, s_ref.shape[0])
    for h in range(nh):
        ot_ref[_head_rows(h), :] = _normalized(acc_ref, h)
    o_ref[...] = ot_ref[...].T.astype(o_ref.dtype)


def _moba(qt, k, vt, kmean, knorm, b, s):
    nq = s // TQ
    return pl.pallas_call(
        functools.partial(_moba_kernel, slopes=_alibi_slopes(0), tq=TQ),
        grid=(b, nq),
        in_specs=[_qt_spec(GROUP_WIDTH, nq), _k_spec(s, GROUP_WIDTH), _vt_spec(nq, GROUP_WIDTH),
                  pl.BlockSpec((1, MOBA_ROWS, GROUP_WIDTH), lambda bb, i: (bb, 0, 0)), _kn_spec(nq)],
        out_specs=_row_spec(GROUP_WIDTH, nq),
        out_shape=jax.ShapeDtypeStruct((b * s, GROUP_WIDTH), BF16),
        scratch_shapes=[pltpu.VMEM((GROUP_HEADS, GROUP_WIDTH, TQ), BF16),
                        pltpu.VMEM((GROUP_HEADS, TQ, TQ), F32),
                        pltpu.VMEM((GROUP_HEADS, MOBA_ROWS, TQ), F32)]
        + _flash_scratch(GROUP_HEADS, SWEEP_GROUP * GROUP_HEADS),
        compiler_params=_params("parallel", "arbitrary"),
        name="moba",
    )(qt, k.reshape(b, s, GROUP_WIDTH), vt, kmean, knorm)


def _nsa_cmp_kernel(x_ref, pa_ref, pb_ref, w1a_ref, w1b_ref, b1_ref, w2_ref, o_ref, *, transposed):
    x = x_ref[0]
    n = x.shape[0]
    a = _mm(x + pa_ref[...], w1a_ref[...], precision=HIGHEST)
    bm = _mm(x + pb_ref[...], w1b_ref[...], precision=HIGHEST)
    hid = jax.nn.gelu(a + pltpu.roll(bm, n - 1, 0) + b1_ref[...], approximate=True)
    if transposed:
        o_ref[0] = _nt(w2_ref[...], hid, precision=HIGHEST)
    else:
        o_ref[0] = _mm(hid, w2_ref[...], precision=HIGHEST)


def _nsa_compress(x, pos, w1, b1, w2, b, s, transposed):
    rows = s // NSA_CMP_STRIDE
    half = NSA_CMP_STRIDE * HEAD_DIM
    hid = w1.shape[-1]
    full = lambda shape: pl.BlockSpec(shape, lambda bb: (0,) * len(shape))
    if transposed:
        w2 = w2.T
        oshape = (b, HEAD_DIM, rows)
    else:
        w2 = jnp.pad(w2, ((0, 0), (0, 128 - HEAD_DIM)))
        oshape = (b, rows, 128)
    return pl.pallas_call(
        functools.partial(_nsa_cmp_kernel, transposed=transposed),
        grid=(b,),
        in_specs=[pl.BlockSpec((1, rows, half), lambda bb: (bb, 0, 0)),
                  full((1, half)), full((1, half)), full((half, hid)), full((half, hid)),
                  full((1, hid)), full(w2.shape)],
        out_specs=pl.BlockSpec((1,) + oshape[1:], lambda bb: (bb, 0, 0)),
        out_shape=jax.ShapeDtypeStruct(oshape, F32),
        compiler_params=_params("parallel"),
        name="nsa_compress",
    )(x.reshape(b, rows, half),
      pos[:NSA_CMP_STRIDE].reshape(1, half), pos[NSA_CMP_STRIDE:].reshape(1, half),
      w1[:NSA_CMP_STRIDE].reshape(half, hid), w1[NSA_CMP_STRIDE:].reshape(half, hid),
      b1.reshape(1, hid), w2)


def _nsa_sel_kernel(qt_ref, kc_ref, vct_ref, ov_ref, oc_ref, sel_ref, sc_ref, ps_ref, *, slopes, tq):
    i = pl.program_id(1)
    scale = HEAD_DIM ** -0.5
    nh = GROUP_HEADS
    nc = kc_ref.shape[1]
    ck = 64
    qt = qt_ref[0].astype(F32)
    kc = kc_ref[0].astype(BF16)
    vct = vct_ref[0].astype(BF16)
    zeros = jnp.zeros((HEAD_DIM, tq), BF16)
    t_row = i * tq + lax.broadcasted_iota(jnp.int32, (1, tq), 1)

    def visible(r, rows):
        cmp_end = (r + lax.broadcasted_iota(jnp.int32, (rows, tq), 0)) * NSA_CMP_STRIDE + (NSA_CMP_LEN - 1)
        return cmp_end, cmp_end <= t_row

    mx = []
    for h in range(nh):
        qs = (qt[_head_rows(h), :] * scale).astype(BF16)
        cmp_end, mask = visible(0, nc)
        sc = _mm(kc, jnp.concatenate([qs, zeros], axis=0)) - slopes[h] * (t_row - cmp_end).astype(F32)
        mx.append(jnp.max(_score(sc_ref, h, jnp.where(mask, sc, NEG_INF)), axis=0, keepdims=True))
    den = []
    for h in range(nh):
        l8 = jnp.zeros((8, tq), F32)
        for r in range(0, nc, ck):
            _, mask = visible(r, ck)
            p = jnp.where(mask, jnp.exp(sc_ref[h, r:r + ck, :] - mx[h]), 0.0)
            sc_ref[h, r:r + ck, :] = p
            l8 = l8 + jnp.sum(p.reshape(ck // 8, 8, tq), axis=0)
        den.append(jnp.maximum(jnp.sum(l8, axis=0, keepdims=True), TINY))
    pieces = [[] for _ in range(nh)]
    for r in range(0, nc, ck):
        tot = None
        for h in range(nh):
            pn = sc_ref[h, r:r + ck, :] / den[h]
            pieces[h].append(pn.astype(BF16))
            tot = pn if tot is None else tot + pn
        ps_ref[r:r + ck, :] = tot
    for h in range(nh):
        oc_ref[0, _head_rows(h), :] = _mm(vct, jnp.concatenate(pieces[h], axis=0))
    imp_t = _mm(ov_ref[...], ps_ref[...], precision=HIGHEST)
    blk = lax.broadcasted_iota(jnp.int32, (SEL_ROWS, tq), 0)
    tt = i * tq + lax.broadcasted_iota(jnp.int32, (SEL_ROWS, tq), 1)
    qblk = lax.shift_right_logical(tt, 6)
    forced = (blk == 0) | (blk == qblk) | (blk == qblk - 1)
    score = jnp.where(forced, FORCED_SCORE, imp_t)
    score = jnp.where(blk * NSA_SEL_BLOCK <= tt, score, NEG_INF)
    sel_ref[0] = _topk_bias_t(score, NSA_SEL_TOPN).astype(sel_ref.dtype)


def _nsa_select(qt, kc, vct, b, s):
    nq = s // TQ
    nc = s // NSA_CMP_STRIDE
    nsel = s // NSA_SEL_BLOCK
    assert nsel <= SEL_ROWS and NSA_SEL_TOPN <= nsel
    cs = np.arange(nc)[None, :] * NSA_CMP_STRIDE
    bs = np.arange(SEL_ROWS)[:, None] * NSA_SEL_BLOCK
    ov = np.clip(np.minimum(cs + NSA_CMP_LEN, bs + NSA_SEL_BLOCK) - np.maximum(cs, bs), 0, None) / NSA_CMP_LEN
    ov[:, nc - 1] = 0.0
    ov[nsel:] = 0.0
    return pl.pallas_call(
        functools.partial(_nsa_sel_kernel, slopes=_alibi_slopes(1), tq=TQ),
        grid=(b, nq),
        in_specs=[_qt_spec(GROUP_WIDTH, nq),
                  pl.BlockSpec((1, nc, 128), lambda bb, i: (bb, 0, 0)),
                  pl.BlockSpec((1, HEAD_DIM, nc), lambda bb, i: (bb, 0, 0)),
                  pl.BlockSpec((SEL_ROWS, nc), lambda bb, i: (0, 0))],
        out_specs=[_qt_spec(GROUP_WIDTH, nq), _qt_spec(SEL_ROWS, nq)],
        out_shape=[jax.ShapeDtypeStruct((b * nq, GROUP_WIDTH, TQ), F32),
                   jax.ShapeDtypeStruct((b * nq, SEL_ROWS, TQ), BF16)],
        scratch_shapes=[pltpu.VMEM((GROUP_HEADS, nc, TQ), F32), pltpu.VMEM((nc, TQ), F32)],
        compiler_params=_params("parallel", "arbitrary"),
        name="nsa_select",
    )(qt, kc, vct, jnp.asarray(ov.astype(np.float32)))


def _nsa_attn_kernel(qt_ref, ks_ref, vst_ref, kw_ref, vwt_ref, sel_ref, oc_ref, g_ref, kn_ref, o_ref,
                     rhs_ref, rhsw_ref, tab_ref, m_ref, acc_ref, s_ref, ot_ref, *, slopes, tq):
    i = pl.program_id(1)
    scale = HEAD_DIM ** -0.5 * LOG2E
    nh = GROUP_HEADS
    qt = qt_ref[0].astype(F32)
    key, qry = _rel_pos(tq)
    rel = (qry - key).astype(F32)
    causal = key <= qry
    frow = lax.broadcasted_iota(jnp.int32, (HEAD_DIM, tq), 0)
    qoff = lax.broadcasted_iota(jnp.int32, (1, tq), 1).astype(F32)
    zeros = jnp.zeros((HEAD_DIM, tq), BF16)
    _flash_init(m_ref, acc_ref)
    ksd = _k_tile(ks_ref, i, tq)
    for h in range(nh):
        qs = (qt[_head_rows(h), :] * scale).astype(BF16)
        slope2 = float(np.float32(slopes[h] * LOG2E))
        s1, s2, s3 = _bf16_pieces(slope2)
        t = qoff * (-slope2)
        t1 = t.astype(BF16).astype(F32)
        t2 = (t - t1).astype(BF16).astype(F32)
        t3 = t - t1 - t2
        feat = jnp.where(frow == 0, s1, jnp.where(frow == 1, s2, jnp.where(frow == 2, s3, jnp.where(
            frow == 3, t1, jnp.where(frow == 4, t2, jnp.where(frow == 5, t3, 0.0))))))
        rhs_ref[h, 0:AUG_BLK0, :] = qs
        rhs_ref[h, AUG_BLK0:AUG_POS0, :] = sel_ref[0]
        rhs_ref[h, AUG_POS0:GROUP_WIDTH, :] = feat.astype(BF16)
        rhsw_ref[h] = jnp.concatenate([qs, zeros], axis=0)
        tab_ref[h] = rel * (-slope2)
    nwin = NSA_WINDOW // tq
    vsd = _with_ones(vst_ref[i])
    mx = [_score(s_ref, h, jnp.where(causal, _mm(ksd, rhs_ref[h]), NEG_INF)) for h in range(nh)]
    win = []
    for back in range(nwin + 1):
        jb = jnp.maximum(i - back, 0)
        kwj = _k_tile(kw_ref, jb, tq)
        for h in range(nh):
            sv = _mm(kwj, rhsw_ref[h]) + tab_ref[h]
            if back == 0:
                sv = jnp.where(causal, sv, NEG_INF)
            elif back == nwin:
                sv = jnp.where(key > qry, sv, NEG_INF)
            win.append(_score(s_ref, (back + 1) * nh + h, sv))
    for h in range(nh):
        _flash_update(s_ref, h, mx[h], 0.0, vsd, h, m_ref, acc_ref)
    for back in range(nwin + 1):
        vwj = _with_ones(vwt_ref[jnp.maximum(i - back, 0)])
        outside = jnp.where(i >= back, 0.0, NEG_INF)
        for h in range(nh):
            _flash_update(s_ref, (back + 1) * nh + h, win[back * nh + h], outside - slopes[h] * LOG2E * tq * back,
                          vwj, nh + h, m_ref, acc_ref)

    kn = jnp.max(kn_ref[...], axis=0)
    nback = [_reach_tiles(rhs_ref[h, 0:AUG_BLK0, :], m_ref[h:h + 1, :], kn[1:2, 0:1], slopes[h] * LOG2E, i, tq)
             for h in range(nh)]

    def make_stage(heads):
        nc = len(heads)

        def score(j, slot):
            ksj = _k_tile(ks_ref, j, tq)
            return tuple(_score(s_ref, slot * nc + n, _mm(ksj, rhs_ref[h])) for n, h in enumerate(heads))

        def absorb(j, slot, mx):
            vsj = _with_ones(vst_ref[j])
            dj = (i - j).astype(F32)
            for n, h in enumerate(heads):
                _flash_update(s_ref, slot * nc + n, mx[n], dj * (-slopes[h] * LOG2E * tq), vsj, h, m_ref, acc_ref)
        return score, absorb, nc

    _segmented_sweep(i, nback, make_stage, s_ref.shape[0])

    for h in range(nh):
        rows = _head_rows(h)
        ot_ref[rows, :] = (g_ref[0, rows, :] * oc_ref[0, rows, :]
                           + g_ref[0, _head_rows(nh + h), :] * _normalized(acc_ref, h)
                           + g_ref[0, _head_rows(2 * nh + h), :] * _normalized(acc_ref, nh + h))
    o_ref[...] = ot_ref[...].T.astype(o_ref.dtype)


def _nsa_attend(qt, ks, vst, kw, vwt, sel, oct, gt, knorm, b, s):
    nq = s // TQ
    return pl.pallas_call(
        functools.partial(_nsa_attn_kernel, slopes=_alibi_slopes(1), tq=TQ),
        grid=(b, nq),
        in_specs=[_qt_spec(GROUP_WIDTH, nq), _k_spec(s, GROUP_WIDTH), _vt_spec(nq, HEAD_DIM),
                  _k_spec(s, 128), _vt_spec(nq, HEAD_DIM), _qt_spec(SEL_ROWS, nq), _qt_spec(GROUP_WIDTH, nq),
                  _qt_spec(3 * GROUP_WIDTH, nq), _kn_spec(nq)],
        out_specs=_row_spec(GROUP_WIDTH, nq),
        out_shape=jax.ShapeDtypeStruct((b * s, GROUP_WIDTH), BF16),
        scratch_shapes=[pltpu.VMEM((GROUP_HEADS, GROUP_WIDTH, TQ), BF16),
                        pltpu.VMEM((GROUP_HEADS, 128, TQ), BF16),
                        pltpu.VMEM((GROUP_HEADS, TQ, TQ), F32)]
        + _flash_scratch(2 * GROUP_HEADS, SWEEP_GROUP * GROUP_HEADS),
        compiler_params=_params("parallel", "arbitrary"),
        name="nsa_attend",
    )(qt, ks.reshape(b, s, GROUP_WIDTH), vst, kw.reshape(b, s, 128), vwt, sel, oct, gt, knorm)


def _diff_kernel(qt_ref, k_ref, vt_ref, lam_ref, sub_ref, kn_ref, o_ref, rhs_ref, tab_ref, m_ref, acc_ref, s_ref, ot_ref,
                 *, slopes, tq, lambda_init):
    i = pl.program_id(1)
    scale = DIFF_QK_DIM ** -0.5 * LOG2E
    nh = GROUP_HEADS
    qt = qt_ref[0].astype(F32)
    key, qry = _rel_pos(tq)
    rel = (qry - key).astype(F32)
    causal = key <= qry
    row = lax.broadcasted_iota(jnp.int32, (GROUP_WIDTH, 1), 0)
    _flash_init(m_ref, acc_ref)
    kd = _k_tile(k_ref, i, tq)
    for h in range(nh):
        lo = h * HEAD_DIM
        rhs_ref[h] = (jnp.where((row >= lo) & (row < lo + DIFF_QK_DIM), qt, 0.0) * scale).astype(BF16)
        rhs_ref[nh + h] = (jnp.where((row >= lo + DIFF_QK_DIM) & (row < lo + HEAD_DIM), qt, 0.0)
                           * scale).astype(BF16)
        tab_ref[h] = rel * (-slopes[h] * LOG2E)
    mx = [_score(s_ref, c, jnp.where(causal, _mm(kd, rhs_ref[c]) + tab_ref[c % nh], NEG_INF))
          for c in range(2 * nh)]
    for c in range(2 * nh):
        _flash_update(s_ref, c, mx[c], 0.0, _with_ones(vt_ref[i, _head_rows(c % nh), :]), c, m_ref, acc_ref)

    kn = jnp.max(kn_ref[...], axis=0)
    nback = [jnp.maximum(
        _reach_tiles(rhs_ref[h], m_ref[h:h + 1, :], kn[2:3, 2 * h:2 * h + 1], slopes[h] * LOG2E, i, tq),
        _reach_tiles(rhs_ref[nh + h], m_ref[nh + h:nh + h + 1, :], kn[2:3, 2 * h + 1:2 * h + 2],
                     slopes[h] * LOG2E, i, tq)) for h in range(nh)]

    def make_stage(heads):
        chains = tuple(heads) + tuple(nh + h for h in heads)
        nc = len(chains)

        def score(j, slot):
            kj = _k_tile(k_ref, j, tq)
            return tuple(_score(s_ref, slot * nc + n, _mm(kj, rhs_ref[c]) + tab_ref[c % nh])
                         for n, c in enumerate(chains))

        def absorb(j, slot, mx):
            dj = (i - j).astype(F32)
            for n, c in enumerate(chains):
                h = c % nh
                _flash_update(s_ref, slot * nc + n, mx[n], dj * (-slopes[h] * LOG2E * tq),
                              _with_ones(vt_ref[j, _head_rows(h), :]), c, m_ref, acc_ref)
        return score, absorb, nc

    _segmented_sweep(i, nback, make_stage, s_ref.shape[0])
    lam_p = lam_ref[...]
    lam = (jnp.exp(jnp.sum(lam_p[0:1] * lam_p[1:2], axis=-1, keepdims=True))
           - jnp.exp(jnp.sum(lam_p[2:3] * lam_p[3:4], axis=-1, keepdims=True)) + lambda_init)
    for h in range(nh):
        o = _normalized(acc_ref, h) - lam * _normalized(acc_ref, nh + h)
        ms = jnp.mean(o * o, axis=0, keepdims=True)
        ot_ref[_head_rows(h), :] = ((o * lax.rsqrt(ms + NORM_EPS)) * sub_ref[...]) * (1.0 - lambda_init)
    o_ref[...] = ot_ref[...].T.astype(o_ref.dtype)


def _diff(qt, k, vt, lam, subln, lambda_init, knorm, b, s):
    nq = s // TQ
    return pl.pallas_call(
        functools.partial(_diff_kernel, slopes=_alibi_slopes(2), tq=TQ, lambda_init=lambda_init),
        grid=(b, nq),
        in_specs=[_qt_spec(GROUP_WIDTH, nq), _k_spec(s, GROUP_WIDTH), _vt_spec(nq, GROUP_WIDTH),
                  pl.BlockSpec((4, DIFF_QK_DIM), lambda bb, i: (0, 0)),
                  pl.BlockSpec((HEAD_DIM, TQ), lambda bb, i: (0, 0)), _kn_spec(nq)],
        out_specs=_row_spec(GROUP_WIDTH, nq),
        out_shape=jax.ShapeDtypeStruct((b * s, GROUP_WIDTH), BF16),
        scratch_shapes=[pltpu.VMEM((2 * GROUP_HEADS, GROUP_WIDTH, TQ), BF16),
                        pltpu.VMEM((GROUP_HEADS, TQ, TQ), F32)]
        + _flash_scratch(2 * GROUP_HEADS, SWEEP_GROUP * 2 * GROUP_HEADS),
        compiler_params=_params("parallel", "arbitrary"),
        name="diff_attn",
    )(qt, k.reshape(b, s, GROUP_WIDTH), vt, lam, jnp.broadcast_to(subln[:, None], (HEAD_DIM, TQ)), knorm)


def _swa_kernel(qt_ref, k_ref, vt_ref, sink_ref, o_ref, rhs_ref, tab_ref, m_ref, acc_ref, s_ref, ot_ref,
                *, slopes, tq):
    i = pl.program_id(1)
    scale = HEAD_DIM ** -0.5 * LOG2E
    nh = GROUP_HEADS
    per_kv = nh // SWA_KV_HEADS
    qt = qt_ref[0].astype(F32)
    key, qry = _rel_pos(tq)
    d = qry - key
    rel = d.astype(F32)
    zeros = jnp.zeros((HEAD_DIM, tq), BF16)
    _flash_init(m_ref, acc_ref)
    kd = _k_tile(k_ref, i, tq)
    for h in range(nh):
        qs = (qt[_head_rows(h), :] * scale).astype(BF16)
        rhs_ref[h] = jnp.concatenate([qs, zeros] if h // per_kv == 0 else [zeros, qs], axis=0)
        tab_ref[h] = rel * (-slopes[h] * LOG2E)
    in_window = (d >= 0) & (d < SWA_WINDOW)
    jp = jnp.maximum(i - 1, 0)
    kp = _k_tile(k_ref, jp, tq)
    mx = [_score(s_ref, h, jnp.where(in_window, _mm(kd, rhs_ref[h]) + tab_ref[h], NEG_INF)) for h in range(nh)]
    mp = [_score(s_ref, nh + h, jnp.where(d + tq < SWA_WINDOW, _mm(kp, rhs_ref[h]) + tab_ref[h], NEG_INF))
          for h in range(nh)]
    outside = jnp.where(i >= 1, 0.0, NEG_INF)
    for h in range(nh):
        _flash_update(s_ref, h, mx[h], 0.0, _with_ones(vt_ref[i, _head_rows(h // per_kv), :]), h, m_ref, acc_ref)
    for h in range(nh):
        _flash_update(s_ref, nh + h, mp[h], outside - slopes[h] * LOG2E * tq,
                      _with_ones(vt_ref[jp, _head_rows(h // per_kv), :]), h, m_ref, acc_ref)

    for h in range(nh):
        sink = sink_ref[h:h + 1, 0:1] * LOG2E
        m_old = m_ref[h:h + 1, :]
        m_new = jnp.maximum(m_old, sink)
        alpha = jnp.exp2(m_old - m_new)
        a = acc_ref[h]
        denom = alpha * a[HEAD_DIM:HEAD_DIM + 1] + jnp.exp2(sink - m_new)
        ot_ref[_head_rows(h), :] = (alpha * a[0:HEAD_DIM]) / denom
    o_ref[...] = ot_ref[...].T.astype(o_ref.dtype)


def _swa(qt, k, vt, sinks, b, s):
    nq = s // TQ
    kvw = SWA_KV_HEADS * HEAD_DIM
    return pl.pallas_call(
        functools.partial(_swa_kernel, slopes=_alibi_slopes(3), tq=TQ),
        grid=(b, nq),
        in_specs=[_qt_spec(GROUP_WIDTH, nq), _k_spec(s, kvw), _vt_spec(nq, kvw),
                  pl.BlockSpec((8, 128), lambda bb, i: (0, 0))],
        out_specs=_row_spec(GROUP_WIDTH, nq),
        out_shape=jax.ShapeDtypeStruct((b * s, GROUP_WIDTH), BF16),
        scratch_shapes=[pltpu.VMEM((GROUP_HEADS, kvw, TQ), BF16),
                        pltpu.VMEM((GROUP_HEADS, TQ, TQ), F32)] + _flash_scratch(GROUP_HEADS, 2 * GROUP_HEADS),
        compiler_params=_params("parallel", "arbitrary"),
        name="swa",
    )(qt, k.reshape(b, s, kvw), vt, jnp.pad(jnp.broadcast_to(sinks[:, None], (GROUP_HEADS, 128)), ((0, 4), (0, 0))))


def _out_proj_kernel(x_ref, a_ref, b_ref, c_ref, d_ref, w_ref, g_ref, o_ref):
    y = None
    for g, m_ref in enumerate((a_ref, b_ref, c_ref, d_ref)):
        part = _mm(m_ref[...], w_ref[g * GROUP_WIDTH:(g + 1) * GROUP_WIDTH, :])
        y = part if y is None else y + part
    o_ref[...] = x_ref[...] + _rms(y, g_ref[...])


def _out_proj(x2, mixes, w, gain):
    n, d = x2.shape
    tm = 512
    row = lambda wd: pl.BlockSpec((tm, wd), lambda i: (i, 0))
    return pl.pallas_call(
        _out_proj_kernel,
        grid=(n // tm,),
        in_specs=[row(d)] + [row(GROUP_WIDTH)] * 4 + [pl.BlockSpec((d, d), lambda i: (0, 0)),
                                                     pl.BlockSpec((1, d), lambda i: (0, 0))],
        out_specs=row(d),
        out_shape=jax.ShapeDtypeStruct((n, d), F32),
        compiler_params=_params("parallel"),
        name="out_proj",
    )(x2, *mixes, w, gain.reshape(1, d))


def _ffn_kernel(x_ref, xp_ref, gpre_ref, wg_ref, wu_ref, cw_ref, cb_ref, wd_ref, gpost_ref, o_ref,
                h_ref, hp_ref, acc_ref, *, tiles_per_seq):
    i = pl.program_id(0)
    f = pl.program_id(1)

    @pl.when(f == 0)
    def _():
        h_ref[...] = _rms(x_ref[...], gpre_ref[...]).astype(BF16)
        hp = _rms(xp_ref[...], gpre_ref[...])
        hp_ref[...] = jnp.where(i % tiles_per_seq != 0, hp, 0.0).astype(BF16)
        acc_ref[...] = jnp.zeros(acc_ref.shape, F32)

    h = h_ref[...]
    a = _mm(h, wg_ref[...])
    ap = _mm(hp_ref[...], wg_ref[...])
    row = lax.broadcasted_iota(jnp.int32, a.shape, 0)
    a1 = jnp.where(row == 0, ap[7:8], pltpu.roll(a, 1, 0))
    a2 = jnp.where(row == 0, ap[6:7], jnp.where(row == 1, ap[7:8], pltpu.roll(a, 2, 0)))
    cw = cw_ref[...]
    conv = cw[0:1] * a2 + cw[1:2] * a1 + cw[2:3] * a + cb_ref[...]
    gated = jax.nn.gelu(conv, approximate=True) * _mm(h, wu_ref[...])
    acc_ref[...] += _mm(gated.astype(BF16), wd_ref[...])

    @pl.when(f == pl.num_programs(1) - 1)
    def _():
        o_ref[...] = x_ref[...] + _rms(acc_ref[...], gpost_ref[...])


def _ffn(x2, gpre, wg, wu, cw, cb, wd, gpost, s):
    n, d = x2.shape
    dff = wg.shape[1]
    tm, tf = 1024, 512
    halo = 8
    return pl.pallas_call(
        functools.partial(_ffn_kernel, tiles_per_seq=s // tm),
        grid=(n // tm, dff // tf),
        in_specs=[pl.BlockSpec((tm, d), lambda i, f: (i, 0)),
                  pl.BlockSpec((halo, d), lambda i, f: (jnp.maximum(i * (tm // halo) - 1, 0), 0)),
                  pl.BlockSpec((1, d), lambda i, f: (0, 0)),
                  pl.BlockSpec((d, tf), lambda i, f: (0, f)),
                  pl.BlockSpec((d, tf), lambda i, f: (0, f)),
                  pl.BlockSpec((CONV_WIDTH, tf), lambda i, f: (0, f)),
                  pl.BlockSpec((1, tf), lambda i, f: (0, f)),
                  pl.BlockSpec((tf, d), lambda i, f: (f, 0)),
                  pl.BlockSpec((1, d), lambda i, f: (0, 0))],
        out_specs=pl.BlockSpec((tm, d), lambda i, f: (i, 0)),
        out_shape=jax.ShapeDtypeStruct((n, d), F32),
        scratch_shapes=[pltpu.VMEM((tm, d), BF16), pltpu.VMEM((halo, d), BF16), pltpu.VMEM((tm, d), F32)],
        compiler_params=_params("parallel", "arbitrary"),
        name="conv_ffn",
    )(x2, x2, gpre.reshape(1, d), wg, wu, cw, cb.reshape(1, dff), wd, gpost.reshape(1, d))


def kernel(x, attn_pre_norm, attn_post_norm, ffn_pre_norm, ffn_post_norm, w_in, w_out, nsa_cmp_pos_k, nsa_cmp_w1_k, nsa_cmp_b1_k, nsa_cmp_w2_k, nsa_cmp_pos_v, nsa_cmp_w1_v, nsa_cmp_b1_v, nsa_cmp_w2_v, diff_lambda_q1, diff_lambda_k1, diff_lambda_q2, diff_lambda_k2, diff_subln, swa_sinks, ffn_w_gate, ffn_w_up, ffn_conv_w, ffn_conv_b, ffn_w_down):
    b, s, d = x.shape
    depth = w_in.shape[0]
    assert s % 1024 == 0 and s // MOBA_BLOCK <= MOBA_ROWS
    ncols, tcols = (jnp.asarray(c) for c in _in_columns())
    x2 = x.reshape(b * s, d)
    for l in range(depth):
        w_ext = jnp.pad(w_in[l], ((0, 0), (0, 1)))
        p = _in_proj(x2, attn_pre_norm[l], jnp.take(w_ext, ncols, axis=1).astype(BF16),
                     jnp.take(w_ext, tcols, axis=1).T.astype(BF16), s)
        kmean = p["kmean"].reshape(b, s // MOBA_BLOCK, GROUP_WIDTH)
        kmean = jnp.pad(kmean, ((0, 0), (0, MOBA_ROWS - s // MOBA_BLOCK), (0, 0)))
        o_moba = _moba(p["mq"], p["mk"], p["mv"], kmean, p["knorm"], b, s)
        kc = _nsa_compress(p["ncmp"][:, :HEAD_DIM], nsa_cmp_pos_k[l], nsa_cmp_w1_k[l], nsa_cmp_b1_k[l],
                           nsa_cmp_w2_k[l], b, s, False)
        vct = _nsa_compress(p["ncmp"][:, HEAD_DIM:], nsa_cmp_pos_v[l], nsa_cmp_w1_v[l], nsa_cmp_b1_v[l],
                            nsa_cmp_w2_v[l], b, s, True)
        o_cmp, sel = _nsa_select(p["nq"], kc, vct, b, s)
        o_nsa = _nsa_attend(p["nq"], p["nks"], p["nvs"], p["nkw"], p["nvw"], sel, o_cmp, p["gates"],
                            p["knorm"], b, s)
        lam = jnp.stack([diff_lambda_q1[l], diff_lambda_k1[l], diff_lambda_q2[l], diff_lambda_k2[l]])
        lambda_init = 0.8 - 0.6 * math.exp(-0.3 * l)
        o_diff = _diff(p["dq"], p["dk"], p["dv"], lam, diff_subln[l], lambda_init, p["knorm"], b, s)
        o_swa = _swa(p["sq"], p["sk"], p["sv"], swa_sinks[l], b, s)
        x2 = _out_proj(x2, (o_moba, o_nsa, o_diff, o_swa), w_out[l].astype(BF16), attn_post_norm[l])
        x2 = _ffn(x2, ffn_pre_norm[l], ffn_w_gate[l].astype(BF16), ffn_w_up[l].astype(BF16), ffn_conv_w[l],
                  ffn_conv_b[l], ffn_w_down[l].astype(BF16), ffn_post_norm[l], s)
    return x2.reshape(b, s, d)
```

```python
import functools
import math

import numpy as np
import jax
import jax.numpy as jnp
from jax import lax
from jax.experimental import pallas as pl
from jax.experimental.pallas import tpu as pltpu

F32 = jnp.float32
BF16 = jnp.bfloat16
HIGHEST = lax.Precision.HIGHEST

N_MIXERS = 4
N_HEADS = 16
GROUP_HEADS = 4
HEAD_DIM = 64
GROUP_WIDTH = 256
MOBA_BLOCK = 256
MOBA_TOPK = 3
NSA_CMP_LEN = 32
NSA_CMP_STRIDE = 16
NSA_SEL_BLOCK = 64
NSA_SEL_TOPN = 16
NSA_WINDOW = 512
DIFF_QK_DIM = 32
SWA_KV_HEADS = 2
SWA_WINDOW = 128
CONV_WIDTH = 3
NORM_EPS = 1e-6
NEG_INF = -1e30
TINY = 1e-30
FORCED_SCORE = 1e4
LOG2E = 1.4426950408889634
ONES_ROWS = 16
UNDERFLOW_MARGIN = 160.0
REMOVED = -3e38

TQ = 256
SWEEP_GROUP = 4
SEL_ROWS = 128
MOBA_ROWS = 32
VMEM_LIMIT = 56 * 1024 * 1024

AUG_BLK0 = HEAD_DIM
AUG_POS0 = AUG_BLK0 + SEL_ROWS
AUG_ONE0 = AUG_POS0 + 3


def _alibi_slopes(mixer):
    slopes = np.power(np.float32(2.0), np.arange(1, N_HEADS + 1, dtype=np.float32) * np.float32(-8.0 / N_HEADS))
    return [float(s) for s in slopes[mixer::N_MIXERS]]


def _bf16_pieces(x):
    out, rem = [], np.float32(x)
    for _ in range(3):
        p = np.float32(np.asarray(rem, np.float32).astype(jnp.bfloat16).astype(np.float32))
        out.append(float(p))
        rem = np.float32(rem - p)
    assert rem == 0.0
    return out


def _nt(a, b, precision=None):
    return lax.dot_general(a, b, (((1,), (1,)), ((), ())), preferred_element_type=F32, precision=precision)


def _mm(a, b, precision=None):
    return jnp.dot(a, b, preferred_element_type=F32, precision=precision)


def _rms(x, gain):
    return (x * lax.rsqrt(jnp.mean(x * x, axis=-1, keepdims=True) + NORM_EPS)) * gain


def _params(*sem):
    return pltpu.CompilerParams(dimension_semantics=sem, vmem_limit_bytes=VMEM_LIMIT)


_N_GROUPS = (("mk", 256, BF16), ("nks", 256, BF16), ("nkw", 128, BF16), ("dk", 256, BF16), ("sk", 128, BF16),
             ("ncmp", 128, F32))
_T_GROUPS = (("mq", 256, BF16), ("nq", 256, BF16), ("dq", 256, BF16), ("sq", 256, BF16),
             ("mv", 256, BF16), ("dv", 256, BF16), ("nvs", 64, BF16), ("nvw", 64, BF16), ("sv", 128, BF16),
             ("gates", 768, F32))
_NORM_ROWS = {"mk": (0, HEAD_DIM), "nks": (1, HEAD_DIM), "dk": (2, DIFF_QK_DIM)}
_ZERO_COL = 2700


def _in_columns():
    r = np.arange
    z = lambda n: np.full(n, _ZERO_COL)
    ncols = np.concatenate([r(256, 512), r(1152, 1216), z(192), r(1280, 1344), z(64), r(1676, 1932),
                            r(2444, 2572), r(1024, 1152)])
    gate = np.asarray([1408 + 3 * h + j for j in range(3) for h in range(4) for _ in range(64)])
    tcols = np.concatenate([r(0, 256), r(768, 1024), r(1420, 1676), r(2188, 2444), r(512, 768), r(1932, 2188),
                            r(1216, 1280), r(1344, 1408), r(2572, 2700), gate])
    assert ncols.shape[0] == sum(w for _, w, _ in _N_GROUPS) and tcols.shape[0] == sum(w for _, w, _ in _T_GROUPS)
    return ncols, tcols


def _in_proj_kernel(x_ref, g_ref, wn_ref, wt_ref, *out_refs, tm, seq):
    names = [n for n, _, _ in _N_GROUPS] + [n for n, _, _ in _T_GROUPS] + ["kmean", "knorm"]
    outs = dict(zip(names, out_refs))
    hb = _rms(x_ref[...], g_ref[...]).astype(BF16)
    off = 0
    for name, width, dtype in _N_GROUPS:
        res = _mm(hb, wn_ref[:, off:off + width])
        off += width
        if name == "mk":
            for r in range(tm // MOBA_BLOCK):
                blk = res[r * MOBA_BLOCK:(r + 1) * MOBA_BLOCK]
                outs["kmean"][0, r:r + 1, :] = jnp.mean(blk, axis=0, keepdims=True)
        if name in _NORM_ROWS:
            nrow, glanes = _NORM_ROWS[name]
            kb = res.astype(BF16).astype(F32)
            il = lax.broadcasted_iota(jnp.int32, (GROUP_WIDTH, 128), 0)
            ic = lax.broadcasted_iota(jnp.int32, (GROUP_WIDTH, 128), 1)
            nrm = _mm((kb * kb).astype(BF16), jnp.where(il // glanes == ic, 1.0, 0.0).astype(BF16))
            for c in range(tm // TQ):
                outs["knorm"][c, nrow:nrow + 1, :] = jnp.max(nrm[c * TQ:(c + 1) * TQ], axis=0, keepdims=True)
        if name == "nks":
            lane = lax.broadcasted_iota(jnp.int32, res.shape, 1)
            row = lax.broadcasted_iota(jnp.int32, res.shape, 0)
            pos = (pl.program_id(0) % (seq // tm)) * tm + row
            onehot = (lane - AUG_BLK0) == lax.shift_right_logical(pos, 6)
            feat = jnp.where((lane >= AUG_POS0) & (lane < AUG_ONE0), (row % TQ).astype(F32),
                             jnp.where((lane >= AUG_ONE0) & (lane < AUG_ONE0 + 3), 1.0, 0.0))
            res = jnp.where(lane < AUG_BLK0, res, jnp.where(lane < AUG_POS0, jnp.where(onehot, 1.0, 0.0), feat))
        outs[name][...] = res.astype(dtype)
    for c in range(tm // TQ):
        outs["knorm"][c, len(_NORM_ROWS):8, :] = jnp.zeros((8 - len(_NORM_ROWS), 128), F32)
    off = 0
    for name, width, dtype in _T_GROUPS:
        res = _nt(wt_ref[off:off + width, :], hb)
        off += width
        if name == "gates":
            res = 1.0 / (1.0 + jnp.exp(-res))
        for c in range(tm // TQ):
            outs[name][c] = res[:, c * TQ:(c + 1) * TQ].astype(dtype)


def _in_proj(x2, gain, wn, wt, seq):
    n, d = x2.shape
    tm = 512
    out_shape = [jax.ShapeDtypeStruct((n, wd), dt) for _, wd, dt in _N_GROUPS]
    out_specs = [pl.BlockSpec((tm, wd), lambda i: (i, 0)) for _, wd, _ in _N_GROUPS]
    out_shape += [jax.ShapeDtypeStruct((n // TQ, wd, TQ), dt) for _, wd, dt in _T_GROUPS]
    out_specs += [pl.BlockSpec((tm // TQ, wd, TQ), lambda i: (i, 0, 0)) for _, wd, _ in _T_GROUPS]
    out_shape.append(jax.ShapeDtypeStruct((n // tm, tm // MOBA_BLOCK, 256), F32))
    out_specs.append(pl.BlockSpec((1, tm // MOBA_BLOCK, 256), lambda i: (i, 0, 0)))
    out_shape.append(jax.ShapeDtypeStruct((n // TQ, 8, 128), F32))
    out_specs.append(pl.BlockSpec((tm // TQ, 8, 128), lambda i: (i, 0, 0)))
    outs = pl.pallas_call(
        functools.partial(_in_proj_kernel, tm=tm, seq=seq),
        grid=(n // tm,),
        in_specs=[pl.BlockSpec((tm, d), lambda i: (i, 0)),
                  pl.BlockSpec((1, d), lambda i: (0, 0)),
                  pl.BlockSpec(wn.shape, lambda i: (0, 0)),
                  pl.BlockSpec(wt.shape, lambda i: (0, 0))],
        out_specs=out_specs,
        out_shape=out_shape,
        compiler_params=_params("parallel"),
        name="in_proj",
    )(x2, gain.reshape(1, d), wn, wt)
    names = [nm for nm, _, _ in _N_GROUPS] + [nm for nm, _, _ in _T_GROUPS] + ["kmean", "knorm"]
    return dict(zip(names, outs))


def _rel_pos(tq):
    key = lax.broadcasted_iota(jnp.int32, (tq, tq), 0)
    qry = lax.broadcasted_iota(jnp.int32, (tq, tq), 1)
    return key, qry


def _head_rows(h, n=1):
    return slice(h * HEAD_DIM, (h + n) * HEAD_DIM)


def _flash_init(m_ref, acc_ref):
    m_ref[...] = jnp.full(m_ref.shape, NEG_INF, F32)
    acc_ref[...] = jnp.zeros(acc_ref.shape, F32)


def _with_ones(vt):
    return jnp.concatenate([vt, jnp.ones((ONES_ROWS, vt.shape[1]), BF16)], axis=0)


def _score(s_ref, c, sv):
    s_ref[c] = sv
    tk, tq = sv.shape
    return jnp.max(sv.reshape(tk // 8, 8, tq), axis=0)


def _flash_update(s_ref, c, mx8, cj, vt, idx, m_ref, acc_ref):
    tk = s_ref.shape[1]
    m_old = m_ref[idx:idx + 1, :]
    m_new = jnp.maximum(m_old, jnp.max(mx8, axis=0, keepdims=True) + cj)
    shift = m_new - cj
    ps = [jnp.exp2(s_ref[c, r * 64:(r + 1) * 64, :] - shift).astype(BF16) for r in range(tk // 64)]
    alpha = jnp.exp2(m_old - m_new)
    acc_ref[idx] = alpha * acc_ref[idx] + _mm(vt, jnp.concatenate(ps, axis=0))
    m_ref[idx:idx + 1, :] = m_new


def _reach_tiles(rhs, m_row, kn2, slope2, i, tq):
    rf = rhs.astype(F32)
    qn2 = jnp.max(jnp.sum(rf * rf, axis=0, keepdims=True), axis=1, keepdims=True)
    bound = jnp.sqrt(qn2 * kn2) * 1.02
    reach = (bound - jnp.min(m_row, axis=1, keepdims=True) + UNDERFLOW_MARGIN) * (1.0 / slope2)
    nb = jnp.floor((reach - 1.0) * (1.0 / tq)) + 1.0
    return jnp.minimum(jnp.maximum(nb, 0.0), i.astype(F32))


def _segmented_sweep(i, nback, make_stage, staged_tiles):
    nh = len(nback)
    starts, prev = [], i
    for h in range(nh):
        st = jnp.minimum(i - nback[h].astype(jnp.int32)[0, 0], prev)
        starts.append(st)
        prev = st
    bounds = starts[::-1] + [i]
    for n in range(1, nh + 1):
        lo, hi = bounds[n - 1], bounds[n]
        score, absorb, per_slot = make_stage(tuple(range(nh - n, nh)))

        def step(j0, size, score=score, absorb=absorb):
            mx = [score(j0 + t, t) for t in range(size)]
            for t in range(size):
                absorb(j0 + t, t, mx[t])

        gsz = SWEEP_GROUP * 2 if n == 1 else SWEEP_GROUP
        assert gsz * per_slot <= staged_tiles

        def group(g, carry, lo=lo, step=step, gsz=gsz):
            step(lo + gsz * g, gsz)
            return carry

        ngroups = (hi - lo) // gsz
        lax.fori_loop(0, ngroups, group, 0)
        rem, done = (hi - lo) % gsz, lo + gsz * ngroups
        size = gsz // 2
        while size >= 1:
            @pl.when((rem // size) % 2 == 1)
            def _(size=size, start=done + (rem // (2 * size)) * (2 * size), step=step):
                step(start, size)
            size //= 2


def _normalized(acc_ref, idx):
    a = acc_ref[idx]
    return a[0:HEAD_DIM] / a[HEAD_DIM:HEAD_DIM + 1]


def _topk_bias_t(score_t, k):
    if score_t.shape[1] > 128:
        return jnp.concatenate([_topk_bias_t(score_t[:, o:o + 128], k) for o in range(0, score_t.shape[1], 128)],
                               axis=1)
    nrow = score_t.shape[0]
    rows = lax.broadcasted_iota(jnp.int32, score_t.shape, 0).astype(F32)
    bias = jnp.full(score_t.shape, NEG_INF, F32)
    work = score_t
    for _ in range(k):
        m = jnp.max(work, axis=0, keepdims=True)
        idx = jnp.min(jnp.where(work == m, rows, float(nrow)), axis=0, keepdims=True)
        pick = rows == idx
        bias = jnp.where(pick, jnp.where(m > 0.5 * NEG_INF, 0.0, NEG_INF), bias)
        work = jnp.where(pick, REMOVED, work)
    return bias


def _k_tile(ref, j, tk):
    return ref[0, pl.ds(pl.multiple_of(j * tk, tk), tk), :]


def _qt_spec(width, nq):
    return pl.BlockSpec((1, width, TQ), lambda b, i: (b * nq + i, 0, 0))


def _kn_spec(nq):
    return pl.BlockSpec((nq, 8, 128), lambda b, i: (b, 0, 0))


def _k_spec(s, width):
    return pl.BlockSpec((1, s, width), lambda b, i: (b, 0, 0))


def _vt_spec(nq, width):
    return pl.BlockSpec((nq, width, TQ), lambda b, i: (b, 0, 0))


def _row_spec(width, nq):
    return pl.BlockSpec((TQ, width), lambda b, i: (b * nq + i, 0))


def _flash_scratch(chains, tiles):
    return [pltpu.VMEM((8, TQ), F32), pltpu.VMEM((chains, HEAD_DIM + ONES_ROWS, TQ), F32),
            pltpu.VMEM((tiles, TQ, TQ), F32), pltpu.VMEM((GROUP_WIDTH, TQ), F32)]


def _moba_kernel(qt_ref, k_ref, vt_ref, km_ref, kn_ref, o_ref, rhs_ref, tab_ref, sel_ref, m_ref, acc_ref, s_ref, ot_ref,
                 *, slopes, tq):
    i = pl.program_id(1)
    nh = GROUP_HEADS
    scale = HEAD_DIM ** -0.5 * LOG2E
    qt = qt_ref[0].astype(F32)
    key, qry = _rel_pos(tq)
    rel = (qry - key).astype(F32)
    causal = key <= qry
    row_head = lax.broadcasted_iota(jnp.int32, (GROUP_WIDTH, 1), 0) // HEAD_DIM
    nblk = km_ref.shape[1]
    blk_rows = lax.broadcasted_iota(jnp.int32, (nblk, tq), 0)
    lane_head = lax.broadcasted_iota(jnp.int32, (1, GROUP_WIDTH), 1) // HEAD_DIM
    km = km_ref[0]
    gates_t = _mm(jnp.concatenate([jnp.where(lane_head == h, km, 0.0) for h in range(nh)], axis=0), qt,
                  precision=HIGHEST)
    _flash_init(m_ref, acc_ref)
    kd = _k_tile(k_ref, i, tq)
    for h in range(nh):
        gate_t = jnp.where(blk_rows < i, gates_t[h * nblk:(h + 1) * nblk], NEG_INF)
        sel_ref[h] = _topk_bias_t(gate_t, MOBA_TOPK)
        rhs_ref[h] = (jnp.where(row_head == h, qt, 0.0) * scale).astype(BF16)
        tab_ref[h] = rel * (-slopes[h] * LOG2E)
    mx = [_score(s_ref, h, jnp.where(causal, _mm(kd, rhs_ref[h]) + tab_ref[h], NEG_INF)) for h in range(nh)]
    for h in range(nh):
        _flash_update(s_ref, h, mx[h], 0.0, _with_ones(vt_ref[i, _head_rows(h), :]), h, m_ref, acc_ref)

    kn = jnp.max(kn_ref[...], axis=0)
    nback = [_reach_tiles(rhs_ref[h], m_ref[h:h + 1, :], kn[0:1, h:h + 1], slopes[h] * LOG2E, i, tq)
             for h in range(nh)]

    def make_stage(heads):
        nc = len(heads)

        def score(j, slot):
            kj = _k_tile(k_ref, j, tq)
            return tuple(_score(s_ref, slot * nc + n, _mm(kj, rhs_ref[h]) + tab_ref[h])
                         for n, h in enumerate(heads))

        def absorb(j, slot, mx):
            dj = (i - j).astype(F32)
            for n, h in enumerate(heads):
                cj = dj * (-slopes[h] * LOG2E * tq) + sel_ref[h, pl.ds(j, 1), :]
                _flash_update(s_ref, slot * nc + n, mx[n], cj, _with_ones(vt_ref[j, _head_rows(h), :]), h,
                              m_ref, acc_ref)
        return score, absorb, nc

    _segmented_sweep(i, nback, make_stage, s_ref.shape[0])
    for h in range(nh):
        ot_ref[_head_rows(h), :] = _normalized(acc_ref, h)
    o_ref[...] = ot_ref[...].T.astype(o_ref.dtype)


def _moba(qt, k, vt, kmean, knorm, b, s):
    nq = s // TQ
    return pl.pallas_call(
        functools.partial(_moba_kernel, slopes=_alibi_slopes(0), tq=TQ),
        grid=(b, nq),
        in_specs=[_qt_spec(GROUP_WIDTH, nq), _k_spec(s, GROUP_WIDTH), _vt_spec(nq, GROUP_WIDTH),
                  pl.BlockSpec((1, MOBA_ROWS, GROUP_WIDTH), lambda bb, i: (bb, 0, 0)), _kn_spec(nq)],
        out_specs=_row_spec(GROUP_WIDTH, nq),
        out_shape=jax.ShapeDtypeStruct((b * s, GROUP_WIDTH), BF16),
        scratch_shapes=[pltpu.VMEM((GROUP_HEADS, GROUP_WIDTH, TQ), BF16),
                        pltpu.VMEM((GROUP_HEADS, TQ, TQ), F32),
                        pltpu.VMEM((GROUP_HEADS, MOBA_ROWS, TQ), F32)]
        + _flash_scratch(GROUP_HEADS, SWEEP_GROUP * GROUP_HEADS),
        compiler_params=_params("parallel", "arbitrary"),
        name="moba",
    )(qt, k.reshape(b, s, GROUP_WIDTH), vt, kmean, knorm)


def _nsa_cmp_kernel(x_ref, pa_ref, pb_ref, w1a_ref, w1b_ref, b1_ref, w2_ref, o_ref, *, transposed):
    x = x_ref[0]
    n = x.shape[0]
    a = _mm(x + pa_ref[...], w1a_ref[...], precision=HIGHEST)
    bm = _mm(x + pb_ref[...], w1b_ref[...], precision=HIGHEST)
    hid = jax.nn.gelu(a + pltpu.roll(bm, n - 1, 0) + b1_ref[...], approximate=True)
    if transposed:
        o_ref[0] = _nt(w2_ref[...], hid, precision=HIGHEST)
    else:
        o_ref[0] = _mm(hid, w2_ref[...], precision=HIGHEST)


def _nsa_compress(x, pos, w1, b1, w2, b, s, transposed):
    rows = s // NSA_CMP_STRIDE
    half = NSA_CMP_STRIDE * HEAD_DIM
    hid = w1.shape[-1]
    full = lambda shape: pl.BlockSpec(shape, lambda bb: (0,) * len(shape))
    if transposed:
        w2 = w2.T
        oshape = (b, HEAD_DIM, rows)
    else:
        w2 = jnp.pad(w2, ((0, 0), (0, 128 - HEAD_DIM)))
        oshape = (b, rows, 128)
    return pl.pallas_call(
        functools.partial(_nsa_cmp_kernel, transposed=transposed),
        grid=(b,),
        in_specs=[pl.BlockSpec((1, rows, half), lambda bb: (bb, 0, 0)),
                  full((1, half)), full((1, half)), full((half, hid)), full((half, hid)),
                  full((1, hid)), full(w2.shape)],
        out_specs=pl.BlockSpec((1,) + oshape[1:], lambda bb: (bb, 0, 0)),
        out_shape=jax.ShapeDtypeStruct(oshape, F32),
        compiler_params=_params("parallel"),
        name="nsa_compress",
    )(x.reshape(b, rows, half),
      pos[:NSA_CMP_STRIDE].reshape(1, half), pos[NSA_CMP_STRIDE:].reshape(1, half),
      w1[:NSA_CMP_STRIDE].reshape(half, hid), w1[NSA_CMP_STRIDE:].reshape(half, hid),
      b1.reshape(1, hid), w2)


def _nsa_sel_kernel(qt_ref, kc_ref, vct_ref, ov_ref, oc_ref, sel_ref, sc_ref, ps_ref, *, slopes, tq):
    i = pl.program_id(1)
    scale = HEAD_DIM ** -0.5
    nh = GROUP_HEADS
    nc = kc_ref.shape[1]
    ck = 64
    qt = qt_ref[0].astype(F32)
    kc = kc_ref[0].astype(BF16)
    vct = vct_ref[0].astype(BF16)
    zeros = jnp.zeros((HEAD_DIM, tq), BF16)
    t_row = i * tq + lax.broadcasted_iota(jnp.int32, (1, tq), 1)

    def visible(r, rows):
        cmp_end = (r + lax.broadcasted_iota(jnp.int32, (rows, tq), 0)) * NSA_CMP_STRIDE + (NSA_CMP_LEN - 1)
        return cmp_end, cmp_end <= t_row

    mx = []
    for h in range(nh):
        qs = (qt[_head_rows(h), :] * scale).astype(BF16)
        cmp_end, mask = visible(0, nc)
        sc = _mm(kc, jnp.concatenate([qs, zeros], axis=0)) - slopes[h] * (t_row - cmp_end).astype(F32)
        mx.append(jnp.max(_score(sc_ref, h, jnp.where(mask, sc, NEG_INF)), axis=0, keepdims=True))
    den = []
    for h in range(nh):
        l8 = jnp.zeros((8, tq), F32)
        for r in range(0, nc, ck):
            _, mask = visible(r, ck)
            p = jnp.where(mask, jnp.exp(sc_ref[h, r:r + ck, :] - mx[h]), 0.0)
            sc_ref[h, r:r + ck, :] = p
            l8 = l8 + jnp.sum(p.reshape(ck // 8, 8, tq), axis=0)
        den.append(jnp.maximum(jnp.sum(l8, axis=0, keepdims=True), TINY))
    pieces = [[] for _ in range(nh)]
    for r in range(0, nc, ck):
        tot = None
        for h in range(nh):
            pn = sc_ref[h, r:r + ck, :] / den[h]
            pieces[h].append(pn.astype(BF16))
            tot = pn if tot is None else tot + pn
        ps_ref[r:r + ck, :] = tot
    for h in range(nh):
        oc_ref[0, _head_rows(h), :] = _mm(vct, jnp.concatenate(pieces[h], axis=0))
    imp_t = _mm(ov_ref[...], ps_ref[...], precision=HIGHEST)
    blk = lax.broadcasted_iota(jnp.int32, (SEL_ROWS, tq), 0)
    tt = i * tq + lax.broadcasted_iota(jnp.int32, (SEL_ROWS, tq), 1)
    qblk = lax.shift_right_logical(tt, 6)
    forced = (blk == 0) | (blk == qblk) | (blk == qblk - 1)
    score = jnp.where(forced, FORCED_SCORE, imp_t)
    score = jnp.where(blk * NSA_SEL_BLOCK <= tt, score, NEG_INF)
    sel_ref[0] = _topk_bias_t(score, NSA_SEL_TOPN).astype(sel_ref.dtype)


def _nsa_select(qt, kc, vct, b, s):
    nq = s // TQ
    nc = s // NSA_CMP_STRIDE
    nsel = s // NSA_SEL_BLOCK
    assert nsel <= SEL_ROWS and NSA_SEL_TOPN <= nsel
    cs = np.arange(nc)[None, :] * NSA_CMP_STRIDE
    bs = np.arange(SEL_ROWS)[:, None] * NSA_SEL_BLOCK
    ov = np.clip(np.minimum(cs + NSA_CMP_LEN, bs + NSA_SEL_BLOCK) - np.maximum(cs, bs), 0, None) / NSA_CMP_LEN
    ov[:, nc - 1] = 0.0
    ov[nsel:] = 0.0
    return pl.pallas_call(
        functools.partial(_nsa_sel_kernel, slopes=_alibi_slopes(1), tq=TQ),
        grid=(b, nq),
        in_specs=[_qt_spec(GROUP_WIDTH, nq),
                  pl.BlockSpec((1, nc, 128), lambda bb, i: (bb, 0, 0)),
                  pl.BlockSpec((1, HEAD_DIM, nc), lambda bb, i: (bb, 0, 0)),
                  pl.BlockSpec((SEL_ROWS, nc), lambda bb, i: (0, 0))],
        out_specs=[_qt_spec(GROUP_WIDTH, nq), _qt_spec(SEL_ROWS, nq)],
        out_shape=[jax.ShapeDtypeStruct((b * nq, GROUP_WIDTH, TQ), F32),
                   jax.ShapeDtypeStruct((b * nq, SEL_ROWS, TQ), BF16)],
        scratch_shapes=[pltpu.VMEM((GROUP_HEADS, nc, TQ), F32), pltpu.VMEM((nc, TQ), F32)],
        compiler_params=_params("parallel", "arbitrary"),
        name="nsa_select",
    )(qt, kc, vct, jnp.asarray(ov.astype(np.float32)))


def _nsa_attn_kernel(qt_ref, ks_ref, vst_ref, kw_ref, vwt_ref, sel_ref, oc_ref, g_ref, kn_ref, o_ref,
                     rhs_ref, rhsw_ref, tab_ref, m_ref, acc_ref, s_ref, ot_ref, *, slopes, tq):
    i = pl.program_id(1)
    scale = HEAD_DIM ** -0.5 * LOG2E
    nh = GROUP_HEADS
    qt = qt_ref[0].astype(F32)
    key, qry = _rel_pos(tq)
    rel = (qry - key).astype(F32)
    causal = key <= qry
    frow = lax.broadcasted_iota(jnp.int32, (HEAD_DIM, tq), 0)
    qoff = lax.broadcasted_iota(jnp.int32, (1, tq), 1).astype(F32)
    zeros = jnp.zeros((HEAD_DIM, tq), BF16)
    _flash_init(m_ref, acc_ref)
    ksd = _k_tile(ks_ref, i, tq)
    for h in range(nh):
        qs = (qt[_head_rows(h), :] * scale).astype(BF16)
        slope2 = float(np.float32(slopes[h] * LOG2E))
        s1, s2, s3 = _bf16_pieces(slope2)
        t = qoff * (-slope2)
        t1 = t.astype(BF16).astype(F32)
        t2 = (t - t1).astype(BF16).astype(F32)
        t3 = t - t1 - t2
        feat = jnp.where(frow == 0, s1, jnp.where(frow == 1, s2, jnp.where(frow == 2, s3, jnp.where(
            frow == 3, t1, jnp.where(frow == 4, t2, jnp.where(frow == 5, t3, 0.0))))))
        rhs_ref[h, 0:AUG_BLK0, :] = qs
        rhs_ref[h, AUG_BLK0:AUG_POS0, :] = sel_ref[0]
        rhs_ref[h, AUG_POS0:GROUP_WIDTH, :] = feat.astype(BF16)
        rhsw_ref[h] = jnp.concatenate([qs, zeros], axis=0)
        tab_ref[h] = rel * (-slope2)
    nwin = NSA_WINDOW // tq
    vsd = _with_ones(vst_ref[i])
    mx = [_score(s_ref, h, jnp.where(causal, _mm(ksd, rhs_ref[h]), NEG_INF)) for h in range(nh)]
    win = []
    for back in range(nwin + 1):
        jb = jnp.maximum(i - back, 0)
        kwj = _k_tile(kw_ref, jb, tq)
        for h in range(nh):
            sv = _mm(kwj, rhsw_ref[h]) + tab_ref[h]
            if back == 0:
                sv = jnp.where(causal, sv, NEG_INF)
            elif back == nwin:
                sv = jnp.where(key > qry, sv, NEG_INF)
            win.append(_score(s_ref, (back + 1) * nh + h, sv))
    for h in range(nh):
        _flash_update(s_ref, h, mx[h], 0.0, vsd, h, m_ref, acc_ref)
    for back in range(nwin + 1):
        vwj = _with_ones(vwt_ref[jnp.maximum(i - back, 0)])
        outside = jnp.where(i >= back, 0.0, NEG_INF)
        for h in range(nh):
            _flash_update(s_ref, (back + 1) * nh + h, win[back * nh + h], outside - slopes[h] * LOG2E * tq * back,
                          vwj, nh + h, m_ref, acc_ref)

    kn = jnp.max(kn_ref[...], axis=0)
    nback = [_reach_tiles(rhs_ref[h, 0:AUG_BLK0, :], m_ref[h:h + 1, :], kn[1:2, 0:1], slopes[h] * LOG2E, i, tq)
             for h in range(nh)]

    def make_stage(heads):
        nc = len(heads)

        def score(j, slot):
            ksj = _k_tile(ks_ref, j, tq)
            return tuple(_score(s_ref, slot * nc + n, _mm(ksj, rhs_ref[h])) for n, h in enumerate(heads))

        def absorb(j, slot, mx):
            vsj = _with_ones(vst_ref[j])
            dj = (i - j).astype(F32)
            for n, h in enumerate(heads):
                _flash_update(s_ref, slot * nc + n, mx[n], dj * (-slopes[h] * LOG2E * tq), vsj, h, m_ref, acc_ref)
        return score, absorb, nc

    _segmented_sweep(i, nback, make_stage, s_ref.shape[0])

    for h in range(nh):
        rows = _head_rows(h)
        ot_ref[rows, :] = (g_ref[0, rows, :] * oc_ref[0, rows, :]
                           + g_ref[0, _head_rows(nh + h), :] * _normalized(acc_ref, h)
                           + g_ref[0, _head_rows(2 * nh + h), :] * _normalized(acc_ref, nh + h))
    o_ref[...] = ot_ref[...].T.astype(o_ref.dtype)


def _nsa_attend(qt, ks, vst, kw, vwt, sel, oct, gt, knorm, b, s):
    nq = s // TQ
    return pl.pallas_call(
        functools.partial(_nsa_attn_kernel, slopes=_alibi_slopes(1), tq=TQ),
        grid=(b, nq),
        in_specs=[_qt_spec(GROUP_WIDTH, nq), _k_spec(s, GROUP_WIDTH), _vt_spec(nq, HEAD_DIM),
                  _k_spec(s, 128), _vt_spec(nq, HEAD_DIM), _qt_spec(SEL_ROWS, nq), _qt_spec(GROUP_WIDTH, nq),
                  _qt_spec(3 * GROUP_WIDTH, nq), _kn_spec(nq)],
        out_specs=_row_spec(GROUP_WIDTH, nq),
        out_shape=jax.ShapeDtypeStruct((b * s, GROUP_WIDTH), BF16),
        scratch_shapes=[pltpu.VMEM((GROUP_HEADS, GROUP_WIDTH, TQ), BF16),
                        pltpu.VMEM((GROUP_HEADS, 128, TQ), BF16),
                        pltpu.VMEM((GROUP_HEADS, TQ, TQ), F32)]
        + _flash_scratch(2 * GROUP_HEADS, SWEEP_GROUP * GROUP_HEADS),
        compiler_params=_params("parallel", "arbitrary"),
        name="nsa_attend",
    )(qt, ks.reshape(b, s, GROUP_WIDTH), vst, kw.reshape(b, s, 128), vwt, sel, oct, gt, knorm)


def _diff_kernel(qt_ref, k_ref, vt_ref, lam_ref, sub_ref, kn_ref, o_ref, rhs_ref, tab_ref, m_ref, acc_ref, s_ref, ot_ref,
                 *, slopes, tq, lambda_init):
    i = pl.program_id(1)
    scale = DIFF_QK_DIM ** -0.5 * LOG2E
    nh = GROUP_HEADS
    qt = qt_ref[0].astype(F32)
    key, qry = _rel_pos(tq)
    rel = (qry - key).astype(F32)
    causal = key <= qry
    row = lax.broadcasted_iota(jnp.int32, (GROUP_WIDTH, 1), 0)
    _flash_init(m_ref, acc_ref)
    kd = _k_tile(k_ref, i, tq)
    for h in range(nh):
        lo = h * HEAD_DIM
        rhs_ref[h] = (jnp.where((row >= lo) & (row < lo + DIFF_QK_DIM), qt, 0.0) * scale).astype(BF16)
        rhs_ref[nh + h] = (jnp.where((row >= lo + DIFF_QK_DIM) & (row < lo + HEAD_DIM), qt, 0.0)
                           * scale).astype(BF16)
        tab_ref[h] = rel * (-slopes[h] * LOG2E)
    mx = [_score(s_ref, c, jnp.where(causal, _mm(kd, rhs_ref[c]) + tab_ref[c % nh], NEG_INF))
          for c in range(2 * nh)]
    for c in range(2 * nh):
        _flash_update(s_ref, c, mx[c], 0.0, _with_ones(vt_ref[i, _head_rows(c % nh), :]), c, m_ref, acc_ref)

    kn = jnp.max(kn_ref[...], axis=0)
    nback = [jnp.maximum(
        _reach_tiles(rhs_ref[h], m_ref[h:h + 1, :], kn[2:3, 2 * h:2 * h + 1], slopes[h] * LOG2E, i, tq),
        _reach_tiles(rhs_ref[nh + h], m_ref[nh + h:nh + h + 1, :], kn[2:3, 2 * h + 1:2 * h + 2],
                     slopes[h] * LOG2E, i, tq)) for h in range(nh)]

    def make_stage(heads):
        chains = tuple(heads) + tuple(nh + h for h in heads)
        nc = len(chains)

        def score(j, slot):
            kj = _k_tile(k_ref, j, tq)
            return tuple(_score(s_ref, slot * nc + n, _mm(kj, rhs_ref[c]) + tab_ref[c % nh])
                         for n, c in enumerate(chains))

        def absorb(j, slot, mx):
            dj = (i - j).astype(F32)
            for n, c in enumerate(chains):
                h = c % nh
                _flash_update(s_ref, slot * nc + n, mx[n], dj * (-slopes[h] * LOG2E * tq),
                              _with_ones(vt_ref[j, _head_rows(h), :]), c, m_ref, acc_ref)
        return score, absorb, nc

    _segmented_sweep(i, nback, make_stage, s_ref.shape[0])
    lam_p = lam_ref[...]
    lam = (jnp.exp(jnp.sum(lam_p[0:1] * lam_p[1:2], axis=-1, keepdims=True))
           - jnp.exp(jnp.sum(lam_p[2:3] * lam_p[3:4], axis=-1, keepdims=True)) + lambda_init)
    for h in range(nh):
        o = _normalized(acc_ref, h) - lam * _normalized(acc_ref, nh + h)
        ms = jnp.mean(o * o, axis=0, keepdims=True)
        ot_ref[_head_rows(h), :] = ((o * lax.rsqrt(ms + NORM_EPS)) * sub_ref[...]) * (1.0 - lambda_init)
    o_ref[...] = ot_ref[...].T.astype(o_ref.dtype)


def _diff(qt, k, vt, lam, subln, lambda_init, knorm, b, s):
    nq = s // TQ
    return pl.pallas_call(
        functools.partial(_diff_kernel, slopes=_alibi_slopes(2), tq=TQ, lambda_init=lambda_init),
        grid=(b, nq),
        in_specs=[_qt_spec(GROUP_WIDTH, nq), _k_spec(s, GROUP_WIDTH), _vt_spec(nq, GROUP_WIDTH),
                  pl.BlockSpec((4, DIFF_QK_DIM), lambda bb, i: (0, 0)),
                  pl.BlockSpec((HEAD_DIM, TQ), lambda bb, i: (0, 0)), _kn_spec(nq)],
        out_specs=_row_spec(GROUP_WIDTH, nq),
        out_shape=jax.ShapeDtypeStruct((b * s, GROUP_WIDTH), BF16),
        scratch_shapes=[pltpu.VMEM((2 * GROUP_HEADS, GROUP_WIDTH, TQ), BF16),
                        pltpu.VMEM((GROUP_HEADS, TQ, TQ), F32)]
        + _flash_scratch(2 * GROUP_HEADS, SWEEP_GROUP * 2 * GROUP_HEADS),
        compiler_params=_params("parallel", "arbitrary"),
        name="diff_attn",
    )(qt, k.reshape(b, s, GROUP_WIDTH), vt, lam, jnp.broadcast_to(subln[:, None], (HEAD_DIM, TQ)), knorm)


def _swa_kernel(qt_ref, k_ref, vt_ref, sink_ref, o_ref, rhs_ref, tab_ref, m_ref, acc_ref, s_ref, ot_ref,
                *, slopes, tq):
    i = pl.program_id(1)
    scale = HEAD_DIM ** -0.5 * LOG2E
    nh = GROUP_HEADS
    per_kv = nh // SWA_KV_HEADS
    qt = qt_ref[0].astype(F32)
    key, qry = _rel_pos(tq)
    d = qry - key
    rel = d.astype(F32)
    zeros = jnp.zeros((HEAD_DIM, tq), BF16)
    _flash_init(m_ref, acc_ref)
    kd = _k_tile(k_ref, i, tq)
    for h in range(nh):
        qs = (qt[_head_rows(h), :] * scale).astype(BF16)
        rhs_ref[h] = jnp.concatenate([qs, zeros] if h // per_kv == 0 else [zeros, qs], axis=0)
        tab_ref[h] = rel * (-slopes[h] * LOG2E)
    in_window = (d >= 0) & (d < SWA_WINDOW)
    jp = jnp.maximum(i - 1, 0)
    kp = _k_tile(k_ref, jp, tq)
    mx = [_score(s_ref, h, jnp.where(in_window, _mm(kd, rhs_ref[h]) + tab_ref[h], NEG_INF)) for h in range(nh)]
    mp = [_score(s_ref, nh + h, jnp.where(d + tq < SWA_WINDOW, _mm(kp, rhs_ref[h]) + tab_ref[h], NEG_INF))
          for h in range(nh)]
    outside = jnp.where(i >= 1, 0.0, NEG_INF)
    for h in range(nh):
        _flash_update(s_ref, h, mx[h], 0.0, _with_ones(vt_ref[i, _head_rows(h // per_kv), :]), h, m_ref, acc_ref)
    for h in range(nh):
        _flash_update(s_ref, nh + h, mp[h], outside - slopes[h] * LOG2E * tq,
                      _with_ones(vt_ref[jp, _head_rows(h // per_kv), :]), h, m_ref, acc_ref)

    for h in range(nh):
        sink = sink_ref[h:h + 1, 0:1] * LOG2E
        m_old = m_ref[h:h + 1, :]
        m_new = jnp.maximum(m_old, sink)
        alpha = jnp.exp2(m_old - m_new)
        a = acc_ref[h]
        denom = alpha * a[HEAD_DIM:HEAD_DIM + 1] + jnp.exp2(sink - m_new)
        ot_ref[_head_rows(h), :] = (alpha * a[0:HEAD_DIM]) / denom
    o_ref[...] = ot_ref[...].T.astype(o_ref.dtype)


def _swa(qt, k, vt, sinks, b, s):
    nq = s // TQ
    kvw = SWA_KV_HEADS * HEAD_DIM
    return pl.pallas_call(
        functools.partial(_swa_kernel, slopes=_alibi_slopes(3), tq=TQ),
        grid=(b, nq),
        in_specs=[_qt_spec(GROUP_WIDTH, nq), _k_spec(s, kvw), _vt_spec(nq, kvw),
                  pl.BlockSpec((8, 128), lambda bb, i: (0, 0))],
        out_specs=_row_spec(GROUP_WIDTH, nq),
        out_shape=jax.ShapeDtypeStruct((b * s, GROUP_WIDTH), BF16),
        scratch_shapes=[pltpu.VMEM((GROUP_HEADS, kvw, TQ), BF16),
                        pltpu.VMEM((GROUP_HEADS, TQ, TQ), F32)] + _flash_scratch(GROUP_HEADS, 2 * GROUP_HEADS),
        compiler_params=_params("parallel", "arbitrary"),
        name="swa",
    )(qt, k.reshape(b, s, kvw), vt, jnp.pad(jnp.broadcast_to(sinks[:, None], (GROUP_HEADS, 128)), ((0, 4), (0, 0))))


def _out_proj_kernel(x_ref, a_ref, b_ref, c_ref, d_ref, w_ref, g_ref, o_ref):
    y = _mm(jnp.concatenate([a_ref[...], b_ref[...], c_ref[...], d_ref[...]], axis=1), w_ref[...])
    o_ref[...] = x_ref[...] + _rms(y, g_ref[...])


def _out_proj(x2, mixes, w, gain):
    n, d = x2.shape
    tm = 512
    row = lambda wd: pl.BlockSpec((tm, wd), lambda i: (i, 0))
    return pl.pallas_call(
        _out_proj_kernel,
        grid=(n // tm,),
        in_specs=[row(d)] + [row(GROUP_WIDTH)] * 4 + [pl.BlockSpec((d, d), lambda i: (0, 0)),
                                                     pl.BlockSpec((1, d), lambda i: (0, 0))],
        out_specs=row(d),
        out_shape=jax.ShapeDtypeStruct((n, d), F32),
        compiler_params=_params("parallel"),
        name="out_proj",
    )(x2, *mixes, w, gain.reshape(1, d))


def _ffn_kernel(x_ref, xp_ref, gpre_ref, wg_ref, wu_ref, cw_ref, cb_ref, wd_ref, gpost_ref, o_ref,
                h_ref, hp_ref, acc_ref, *, tiles_per_seq):
    i = pl.program_id(0)
    f = pl.program_id(1)

    @pl.when(f == 0)
    def _():
        h_ref[...] = _rms(x_ref[...], gpre_ref[...]).astype(BF16)
        hp = _rms(xp_ref[...], gpre_ref[...])
        hp_ref[...] = jnp.where(i % tiles_per_seq != 0, hp, 0.0).astype(BF16)
        acc_ref[...] = jnp.zeros(acc_ref.shape, F32)

    h = h_ref[...]
    a = _mm(h, wg_ref[...])
    ap = _mm(hp_ref[...], wg_ref[...])
    row = lax.broadcasted_iota(jnp.int32, a.shape, 0)
    a1 = jnp.where(row == 0, ap[7:8], pltpu.roll(a, 1, 0))
    a2 = jnp.where(row == 0, ap[6:7], jnp.where(row == 1, ap[7:8], pltpu.roll(a, 2, 0)))
    cw = cw_ref[...]
    conv = cw[0:1] * a2 + cw[1:2] * a1 + cw[2:3] * a + cb_ref[...]
    gated = jax.nn.gelu(conv, approximate=True) * _mm(h, wu_ref[...])
    acc_ref[...] += _mm(gated.astype(BF16), wd_ref[...])

    @pl.when(f == pl.num_programs(1) - 1)
    def _():
        o_ref[...] = x_ref[...] + _rms(acc_ref[...], gpost_ref[...])


def _ffn(x2, gpre, wg, wu, cw, cb, wd, gpost, s):
    n, d = x2.shape
    dff = wg.shape[1]
    tm, tf = 1024, 1024
    halo = 8
    return pl.pallas_call(
        functools.partial(_ffn_kernel, tiles_per_seq=s // tm),
        grid=(n // tm, dff // tf),
        in_specs=[pl.BlockSpec((tm, d), lambda i, f: (i, 0)),
                  pl.BlockSpec((halo, d), lambda i, f: (jnp.maximum(i * (tm // halo) - 1, 0), 0)),
                  pl.BlockSpec((1, d), lambda i, f: (0, 0)),
                  pl.BlockSpec((d, tf), lambda i, f: (0, f)),
                  pl.BlockSpec((d, tf), lambda i, f: (0, f)),
                  pl.BlockSpec((CONV_WIDTH, tf), lambda i, f: (0, f)),
                  pl.BlockSpec((1, tf), lambda i, f: (0, f)),
                  pl.BlockSpec((tf, d), lambda i, f: (f, 0)),
                  pl.BlockSpec((1, d), lambda i, f: (0, 0))],
        out_specs=pl.BlockSpec((tm, d), lambda i, f: (i, 0)),
        out_shape=jax.ShapeDtypeStruct((n, d), F32),
        scratch_shapes=[pltpu.VMEM((tm, d), BF16), pltpu.VMEM((halo, d), BF16), pltpu.VMEM((tm, d), F32)],
        compiler_params=_params("parallel", "arbitrary"),
        name="conv_ffn",
    )(x2, x2, gpre.reshape(1, d), wg, wu, cw, cb.reshape(1, dff), wd, gpost.reshape(1, d))


def kernel(x, attn_pre_norm, attn_post_norm, ffn_pre_norm, ffn_post_norm, w_in, w_out, nsa_cmp_pos_k, nsa_cmp_w1_k, nsa_cmp_b1_k, nsa_cmp_w2_k, nsa_cmp_pos_v, nsa_cmp_w1_v, nsa_cmp_b1_v, nsa_cmp_w2_v, diff_lambda_q1, diff_lambda_k1, diff_lambda_q2, diff_lambda_k2, diff_subln, swa_sinks, ffn_w_gate, ffn_w_up, ffn_conv_w, ffn_conv_b, ffn_w_down):
    b, s, d = x.shape
    depth = w_in.shape[0]
    assert s % 1024 == 0 and s // MOBA_BLOCK <= MOBA_ROWS
    ncols, tcols = (jnp.asarray(c) for c in _in_columns())
    x2 = x.reshape(b * s, d)
    for l in range(depth):
        w_ext = jnp.pad(w_in[l], ((0, 0), (0, 1)))
        p = _in_proj(x2, attn_pre_norm[l], jnp.take(w_ext, ncols, axis=1).astype(BF16),
                     jnp.take(w_ext, tcols, axis=1).T.astype(BF16), s)
        kmean = p["kmean"].reshape(b, s // MOBA_BLOCK, GROUP_WIDTH)
        kmean = jnp.pad(kmean, ((0, 0), (0, MOBA_ROWS - s // MOBA_BLOCK), (0, 0)))
        o_moba = _moba(p["mq"], p["mk"], p["mv"], kmean, p["knorm"], b, s)
        kc = _nsa_compress(p["ncmp"][:, :HEAD_DIM], nsa_cmp_pos_k[l], nsa_cmp_w1_k[l], nsa_cmp_b1_k[l],
                           nsa_cmp_w2_k[l], b, s, False)
        vct = _nsa_compress(p["ncmp"][:, HEAD_DIM:], nsa_cmp_pos_v[l], nsa_cmp_w1_v[l], nsa_cmp_b1_v[l],
                            nsa_cmp_w2_v[l], b, s, True)
        o_cmp, sel = _nsa_select(p["nq"], kc, vct, b, s)
        o_nsa = _nsa_attend(p["nq"], p["nks"], p["nvs"], p["nkw"], p["nvw"], sel, o_cmp, p["gates"],
                            p["knorm"], b, s)
        lam = jnp.stack([diff_lambda_q1[l], diff_lambda_k1[l], diff_lambda_q2[l], diff_lambda_k2[l]])
        lambda_init = 0.8 - 0.6 * math.exp(-0.3 * l)
        o_diff = _diff(p["dq"], p["dk"], p["dv"], lam, diff_subln[l], lambda_init, p["knorm"], b, s)
        o_swa = _swa(p["sq"], p["sk"], p["sv"], swa_sinks[l], b, s)
        x2 = _out_proj(x2, (o_moba, o_nsa, o_diff, o_swa), w_out[l].astype(BF16), attn_post_norm[l])
        x2 = _ffn(x2, ffn_pre_norm[l], ffn_w_gate[l].astype(BF16), ffn_w_up[l].astype(BF16), ffn_conv_w[l],
                  ffn_conv_b[l], ffn_w_down[l].astype(BF16), ffn_post_norm[l], s)
    return x2.reshape(b, s, d)
```

```python
import functools
import math

import numpy as np
import jax
import jax.numpy as jnp
from jax import lax
from jax.experimental import pallas as pl
from jax.experimental.pallas import tpu as pltpu

F32 = jnp.float32
BF16 = jnp.bfloat16
HIGHEST = lax.Precision.HIGHEST

N_MIXERS = 4
N_HEADS = 16
GROUP_HEADS = 4
HEAD_DIM = 64
GROUP_WIDTH = 256
MOBA_BLOCK = 256
MOBA_TOPK = 3
NSA_CMP_LEN = 32
NSA_CMP_STRIDE = 16
NSA_SEL_BLOCK = 64
NSA_SEL_TOPN = 16
NSA_WINDOW = 512
DIFF_QK_DIM = 32
SWA_KV_HEADS = 2
SWA_WINDOW = 128
CONV_WIDTH = 3
NORM_EPS = 1e-6
NEG_INF = -1e30
TINY = 1e-30
FORCED_SCORE = 1e4
LOG2E = 1.4426950408889634
ONES_ROWS = 16
UNDERFLOW_MARGIN = 160.0
REMOVED = -3e38

TQ = 256
SWEEP_GROUP = 4
SEL_ROWS = 128
MOBA_ROWS = 32
VMEM_LIMIT = 56 * 1024 * 1024

AUG_BLK0 = HEAD_DIM
AUG_POS0 = AUG_BLK0 + SEL_ROWS
AUG_ONE0 = AUG_POS0 + 3


def _alibi_slopes(mixer):
    slopes = np.power(np.float32(2.0), np.arange(1, N_HEADS + 1, dtype=np.float32) * np.float32(-8.0 / N_HEADS))
    return [float(s) for s in slopes[mixer::N_MIXERS]]


def _bf16_pieces(x):
    out, rem = [], np.float32(x)
    for _ in range(3):
        p = np.float32(np.asarray(rem, np.float32).astype(jnp.bfloat16).astype(np.float32))
        out.append(float(p))
        rem = np.float32(rem - p)
    assert rem == 0.0
    return out


def _nt(a, b, precision=None):
    return lax.dot_general(a, b, (((1,), (1,)), ((), ())), preferred_element_type=F32, precision=precision)


def _mm(a, b, precision=None):
    return jnp.dot(a, b, preferred_element_type=F32, precision=precision)


def _rms(x, gain):
    return (x * lax.rsqrt(jnp.mean(x * x, axis=-1, keepdims=True) + NORM_EPS)) * gain


def _params(*sem):
    return pltpu.CompilerParams(dimension_semantics=sem, vmem_limit_bytes=VMEM_LIMIT)


_N_GROUPS = (("mk", 256, BF16), ("nks", 256, BF16), ("nkw", 128, BF16), ("dk", 256, BF16), ("sk", 128, BF16),
             ("ncmp", 128, F32))
_T_GROUPS = (("mq", 256, BF16), ("nq", 256, BF16), ("dq", 256, BF16), ("sq", 256, BF16),
             ("mv", 256, BF16), ("dv", 256, BF16), ("nvs", 64, BF16), ("nvw", 64, BF16), ("sv", 128, BF16),
             ("gates", 768, F32))
_NORM_ROWS = {"mk": (0, HEAD_DIM), "nks": (1, HEAD_DIM), "dk": (2, DIFF_QK_DIM)}
_ZERO_COL = 2700


def _in_columns():
    r = np.arange
    z = lambda n: np.full(n, _ZERO_COL)
    ncols = np.concatenate([r(256, 512), r(1152, 1216), z(192), r(1280, 1344), z(64), r(1676, 1932),
                            r(2444, 2572), r(1024, 1152)])
    gate = np.asarray([1408 + 3 * h + j for j in range(3) for h in range(4) for _ in range(64)])
    tcols = np.concatenate([r(0, 256), r(768, 1024), r(1420, 1676), r(2188, 2444), r(512, 768), r(1932, 2188),
                            r(1216, 1280), r(1344, 1408), r(2572, 2700), gate])
    assert ncols.shape[0] == sum(w for _, w, _ in _N_GROUPS) and tcols.shape[0] == sum(w for _, w, _ in _T_GROUPS)
    return ncols, tcols


def _in_proj_kernel(x_ref, g_ref, wn_ref, wt_ref, *out_refs, tm, seq):
    names = [n for n, _, _ in _N_GROUPS] + [n for n, _, _ in _T_GROUPS] + ["kmean", "knorm"]
    outs = dict(zip(names, out_refs))
    hb = _rms(x_ref[...], g_ref[...]).astype(BF16)
    off = 0
    for name, width, dtype in _N_GROUPS:
        res = _mm(hb, wn_ref[:, off:off + width])
        off += width
        if name == "mk":
            for r in range(tm // MOBA_BLOCK):
                blk = res[r * MOBA_BLOCK:(r + 1) * MOBA_BLOCK]
                outs["kmean"][0, r:r + 1, :] = jnp.mean(blk, axis=0, keepdims=True)
        if name in _NORM_ROWS:
            nrow, glanes = _NORM_ROWS[name]
            kb = res.astype(BF16).astype(F32)
            il = lax.broadcasted_iota(jnp.int32, (GROUP_WIDTH, 128), 0)
            ic = lax.broadcasted_iota(jnp.int32, (GROUP_WIDTH, 128), 1)
            nrm = _mm((kb * kb).astype(BF16), jnp.where(il // glanes == ic, 1.0, 0.0).astype(BF16))
            for c in range(tm // TQ):
                outs["knorm"][c, nrow:nrow + 1, :] = jnp.max(nrm[c * TQ:(c + 1) * TQ], axis=0, keepdims=True)
        if name == "nks":
            lane = lax.broadcasted_iota(jnp.int32, res.shape, 1)
            row = lax.broadcasted_iota(jnp.int32, res.shape, 0)
            pos = (pl.program_id(0) % (seq // tm)) * tm + row
            onehot = (lane - AUG_BLK0) == lax.shift_right_logical(pos, 6)
            feat = jnp.where((lane >= AUG_POS0) & (lane < AUG_ONE0), (row % TQ).astype(F32),
                             jnp.where((lane >= AUG_ONE0) & (lane < AUG_ONE0 + 3), 1.0, 0.0))
            res = jnp.where(lane < AUG_BLK0, res, jnp.where(lane < AUG_POS0, jnp.where(onehot, 1.0, 0.0), feat))
        outs[name][...] = res.astype(dtype)
    for c in range(tm // TQ):
        outs["knorm"][c, len(_NORM_ROWS):8, :] = jnp.zeros((8 - len(_NORM_ROWS), 128), F32)
    off = 0
    for name, width, dtype in _T_GROUPS:
        res = _nt(wt_ref[off:off + width, :], hb)
        off += width
        if name == "gates":
            res = 1.0 / (1.0 + jnp.exp(-res))
        for c in range(tm // TQ):
            outs[name][c] = res[:, c * TQ:(c + 1) * TQ].astype(dtype)


def _in_proj(x2, gain, wn, wt, seq):
    n, d = x2.shape
    tm = 1024
    out_shape = [jax.ShapeDtypeStruct((n, wd), dt) for _, wd, dt in _N_GROUPS]
    out_specs = [pl.BlockSpec((tm, wd), lambda i: (i, 0)) for _, wd, _ in _N_GROUPS]
    out_shape += [jax.ShapeDtypeStruct((n // TQ, wd, TQ), dt) for _, wd, dt in _T_GROUPS]
    out_specs += [pl.BlockSpec((tm // TQ, wd, TQ), lambda i: (i, 0, 0)) for _, wd, _ in _T_GROUPS]
    out_shape.append(jax.ShapeDtypeStruct((n // tm, tm // MOBA_BLOCK, 256), F32))
    out_specs.append(pl.BlockSpec((1, tm // MOBA_BLOCK, 256), lambda i: (i, 0, 0)))
    out_shape.append(jax.ShapeDtypeStruct((n // TQ, 8, 128), F32))
    out_specs.append(pl.BlockSpec((tm // TQ, 8, 128), lambda i: (i, 0, 0)))
    outs = pl.pallas_call(
        functools.partial(_in_proj_kernel, tm=tm, seq=seq),
        grid=(n // tm,),
        in_specs=[pl.BlockSpec((tm, d), lambda i: (i, 0)),
                  pl.BlockSpec((1, d), lambda i: (0, 0)),
                  pl.BlockSpec(wn.shape, lambda i: (0, 0)),
                  pl.BlockSpec(wt.shape, lambda i: (0, 0))],
        out_specs=out_specs,
        out_shape=out_shape,
        compiler_params=_params("parallel"),
        name="in_proj",
    )(x2, gain.reshape(1, d), wn, wt)
    names = [nm for nm, _, _ in _N_GROUPS] + [nm for nm, _, _ in _T_GROUPS] + ["kmean", "knorm"]
    return dict(zip(names, outs))


def _rel_pos(tq):
    key = lax.broadcasted_iota(jnp.int32, (tq, tq), 0)
    qry = lax.broadcasted_iota(jnp.int32, (tq, tq), 1)
    return key, qry


def _head_rows(h, n=1):
    return slice(h * HEAD_DIM, (h + n) * HEAD_DIM)


def _flash_init(m_ref, acc_ref):
    m_ref[...] = jnp.full(m_ref.shape, NEG_INF, F32)
    acc_ref[...] = jnp.zeros(acc_ref.shape, F32)


def _with_ones(vt):
    return jnp.concatenate([vt, jnp.ones((ONES_ROWS, vt.shape[1]), BF16)], axis=0)


def _score(s_ref, c, sv):
    s_ref[c] = sv
    tk, tq = sv.shape
    return jnp.max(sv.reshape(tk // 8, 8, tq), axis=0)


def _flash_update(s_ref, c, mx8, cj, vt, idx, m_ref, acc_ref):
    tk = s_ref.shape[1]
    m_old = m_ref[idx:idx + 1, :]
    m_new = jnp.maximum(m_old, jnp.max(mx8, axis=0, keepdims=True) + cj)
    shift = m_new - cj
    ps = [jnp.exp2(s_ref[c, r * 64:(r + 1) * 64, :] - shift).astype(BF16) for r in range(tk // 64)]
    alpha = jnp.exp2(m_old - m_new)
    acc_ref[idx] = alpha * acc_ref[idx] + _mm(vt, jnp.concatenate(ps, axis=0))
    m_ref[idx:idx + 1, :] = m_new


def _reach_tiles(rhs, m_row, kn2, slope2, i, tq):
    rf = rhs.astype(F32)
    qn2 = jnp.max(jnp.sum(rf * rf, axis=0, keepdims=True), axis=1, keepdims=True)
    bound = jnp.sqrt(qn2 * kn2) * 1.02
    reach = (bound - jnp.min(m_row, axis=1, keepdims=True) + UNDERFLOW_MARGIN) * (1.0 / slope2)
    nb = jnp.floor((reach - 1.0) * (1.0 / tq)) + 1.0
    return jnp.minimum(jnp.maximum(nb, 0.0), i.astype(F32))


def _segmented_sweep(i, nback, make_stage, staged_tiles):
    nh = len(nback)
    starts, prev = [], i
    for h in range(nh):
        st = jnp.minimum(i - nback[h].astype(jnp.int32)[0, 0], prev)
        starts.append(st)
        prev = st
    bounds = starts[::-1] + [i]
    for n in range(1, nh + 1):
        lo, hi = bounds[n - 1], bounds[n]
        score, absorb, per_slot = make_stage(tuple(range(nh - n, nh)))

        def step(j0, size, score=score, absorb=absorb):
            mx = [score(j0 + t, t) for t in range(size)]
            for t in range(size):
                absorb(j0 + t, t, mx[t])

        gsz = SWEEP_GROUP * 2 if n <= 2 else SWEEP_GROUP
        assert gsz * per_slot <= staged_tiles

        def group(g, carry, lo=lo, step=step, gsz=gsz):
            step(lo + gsz * g, gsz)
            return carry

        ngroups = (hi - lo) // gsz
        lax.fori_loop(0, ngroups, group, 0)
        rem, done = (hi - lo) % gsz, lo + gsz * ngroups
        size = gsz // 2
        while size >= 1:
            @pl.when((rem // size) % 2 == 1)
            def _(size=size, start=done + (rem // (2 * size)) * (2 * size), step=step):
                step(start, size)
            size //= 2


def _normalized(acc_ref, idx):
    a = acc_ref[idx]
    return a[0:HEAD_DIM] / a[HEAD_DIM:HEAD_DIM + 1]


def _topk_bias_t(score_t, k):
    if score_t.shape[1] > 128:
        return jnp.concatenate([_topk_bias_t(score_t[:, o:o + 128], k) for o in range(0, score_t.shape[1], 128)],
                               axis=1)
    nrow = score_t.shape[0]
    rows = lax.broadcasted_iota(jnp.int32, score_t.shape, 0).astype(F32)
    bias = jnp.full(score_t.shape, NEG_INF, F32)
    work = score_t
    for _ in range(k):
        m = jnp.max(work, axis=0, keepdims=True)
        idx = jnp.min(jnp.where(work == m, rows, float(nrow)), axis=0, keepdims=True)
        pick = rows == idx
        bias = jnp.where(pick, jnp.where(m > 0.5 * NEG_INF, 0.0, NEG_INF), bias)
        work = jnp.where(pick, REMOVED, work)
    return bias


def _k_tile(ref, j, tk):
    return ref[0, pl.ds(pl.multiple_of(j * tk, tk), tk), :]


def _qt_spec(width, nq):
    return pl.BlockSpec((1, width, TQ), lambda b, i: (b * nq + i, 0, 0))


def _kn_spec(nq):
    return pl.BlockSpec((nq, 8, 128), lambda b, i: (b, 0, 0))


def _k_spec(s, width):
    return pl.BlockSpec((1, s, width), lambda b, i: (b, 0, 0))


def _vt_spec(nq, width):
    return pl.BlockSpec((nq, width, TQ), lambda b, i: (b, 0, 0))


def _row_spec(width, nq):
    return pl.BlockSpec((TQ, width), lambda b, i: (b * nq + i, 0))


def _flash_scratch(chains, tiles):
    return [pltpu.VMEM((8, TQ), F32), pltpu.VMEM((chains, HEAD_DIM + ONES_ROWS, TQ), F32),
            pltpu.VMEM((tiles, TQ, TQ), F32), pltpu.VMEM((GROUP_WIDTH, TQ), F32)]


def _moba_kernel(qt_ref, k_ref, vt_ref, km_ref, kn_ref, o_ref, rhs_ref, tab_ref, sel_ref, m_ref, acc_ref, s_ref, ot_ref,
                 *, slopes, tq):
    i = pl.program_id(1)
    nh = GROUP_HEADS
    scale = HEAD_DIM ** -0.5 * LOG2E
    qt = qt_ref[0].astype(F32)
    key, qry = _rel_pos(tq)
    rel = (qry - key).astype(F32)
    causal = key <= qry
    row_head = lax.broadcasted_iota(jnp.int32, (GROUP_WIDTH, 1), 0) // HEAD_DIM
    nblk = km_ref.shape[1]
    blk_rows = lax.broadcasted_iota(jnp.int32, (nblk, tq), 0)
    lane_head = lax.broadcasted_iota(jnp.int32, (1, GROUP_WIDTH), 1) // HEAD_DIM
    km = km_ref[0]
    gates_t = _mm(jnp.concatenate([jnp.where(lane_head == h, km, 0.0) for h in range(nh)], axis=0), qt,
                  precision=HIGHEST)
    _flash_init(m_ref, acc_ref)
    kd = _k_tile(k_ref, i, tq)
    for h in range(nh):
        gate_t = jnp.where(blk_rows < i, gates_t[h * nblk:(h + 1) * nblk], NEG_INF)
        sel_ref[h] = _topk_bias_t(gate_t, MOBA_TOPK)
        rhs_ref[h] = (jnp.where(row_head == h, qt, 0.0) * scale).astype(BF16)
        tab_ref[h] = rel * (-slopes[h] * LOG2E)
    mx = [_score(s_ref, h, jnp.where(causal, _mm(kd, rhs_ref[h]) + tab_ref[h], NEG_INF)) for h in range(nh)]
    for h in range(nh):
        _flash_update(s_ref, h, mx[h], 0.0, _with_ones(vt_ref[i, _head_rows(h), :]), h, m_ref, acc_ref)

    kn = jnp.max(kn_ref[...], axis=0)
    nback = [_reach_tiles(rhs_ref[h], m_ref[h:h + 1, :], kn[0:1, h:h + 1], slopes[h] * LOG2E, i, tq)
             for h in range(nh)]

    def make_stage(heads):
        nc = len(heads)

        def score(j, slot):
            kj = _k_tile(k_ref, j, tq)
            return tuple(_score(s_ref, slot * nc + n, _mm(kj, rhs_ref[h]) + tab_ref[h])
                         for n, h in enumerate(heads))

        def absorb(j, slot, mx):
            dj = (i - j).astype(F32)
            for n, h in enumerate(heads):
                cj = dj * (-slopes[h] * LOG2E * tq) + sel_ref[h, pl.ds(j, 1), :]
                _flash_update(s_ref, slot * nc + n, mx[n], cj, _with_ones(vt_ref[j, _head_rows(h), :]), h,
                              m_ref, acc_ref)
        return score, absorb, nc

    _segmented_sweep(i, nback, make_stage, s_ref.shape[0])
    for h in range(nh):
        ot_ref[_head_rows(h), :] = _normalized(acc_ref, h)
    o_ref[...] = ot_ref[...].T.astype(o_ref.dtype)


def _moba(qt, k, vt, kmean, knorm, b, s):
    nq = s // TQ
    return pl.pallas_call(
        functools.partial(_moba_kernel, slopes=_alibi_slopes(0), tq=TQ),
        grid=(b, nq),
        in_specs=[_qt_spec(GROUP_WIDTH, nq), _k_spec(s, GROUP_WIDTH), _vt_spec(nq, GROUP_WIDTH),
                  pl.BlockSpec((1, MOBA_ROWS, GROUP_WIDTH), lambda bb, i: (bb, 0, 0)), _kn_spec(nq)],
        out_specs=_row_spec(GROUP_WIDTH, nq),
        out_shape=jax.ShapeDtypeStruct((b * s, GROUP_WIDTH), BF16),
        scratch_shapes=[pltpu.VMEM((GROUP_HEADS, GROUP_WIDTH, TQ), BF16),
                        pltpu.VMEM((GROUP_HEADS, TQ, TQ), F32),
                        pltpu.VMEM((GROUP_HEADS, MOBA_ROWS, TQ), F32)]
        + _flash_scratch(GROUP_HEADS, SWEEP_GROUP * GROUP_HEADS),
        compiler_params=_params("parallel", "arbitrary"),
        name="moba",
    )(qt, k.reshape(b, s, GROUP_WIDTH), vt, kmean, knorm)


def _nsa_cmp_kernel(x_ref, pa_ref, pb_ref, w1a_ref, w1b_ref, b1_ref, w2_ref, o_ref, *, transposed):
    x = x_ref[0]
    n = x.shape[0]
    a = _mm(x + pa_ref[...], w1a_ref[...], precision=HIGHEST)
    bm = _mm(x + pb_ref[...], w1b_ref[...], precision=HIGHEST)
    hid = jax.nn.gelu(a + pltpu.roll(bm, n - 1, 0) + b1_ref[...], approximate=True)
    if transposed:
        o_ref[0] = _nt(w2_ref[...], hid, precision=HIGHEST)
    else:
        o_ref[0] = _mm(hid, w2_ref[...], precision=HIGHEST)


def _nsa_compress(x, pos, w1, b1, w2, b, s, transposed):
    rows = s // NSA_CMP_STRIDE
    half = NSA_CMP_STRIDE * HEAD_DIM
    hid = w1.shape[-1]
    full = lambda shape: pl.BlockSpec(shape, lambda bb: (0,) * len(shape))
    if transposed:
        w2 = w2.T
        oshape = (b, HEAD_DIM, rows)
    else:
        w2 = jnp.pad(w2, ((0, 0), (0, 128 - HEAD_DIM)))
        oshape = (b, rows, 128)
    return pl.pallas_call(
        functools.partial(_nsa_cmp_kernel, transposed=transposed),
        grid=(b,),
        in_specs=[pl.BlockSpec((1, rows, half), lambda bb: (bb, 0, 0)),
                  full((1, half)), full((1, half)), full((half, hid)), full((half, hid)),
                  full((1, hid)), full(w2.shape)],
        out_specs=pl.BlockSpec((1,) + oshape[1:], lambda bb: (bb, 0, 0)),
        out_shape=jax.ShapeDtypeStruct(oshape, F32),
        compiler_params=_params("parallel"),
        name="nsa_compress",
    )(x.reshape(b, rows, half),
      pos[:NSA_CMP_STRIDE].reshape(1, half), pos[NSA_CMP_STRIDE:].reshape(1, half),
      w1[:NSA_CMP_STRIDE].reshape(half, hid), w1[NSA_CMP_STRIDE:].reshape(half, hid),
      b1.reshape(1, hid), w2)


def _nsa_sel_kernel(qt_ref, kc_ref, vct_ref, ov_ref, oc_ref, sel_ref, sc_ref, ps_ref, *, slopes, tq):
    i = pl.program_id(1)
    scale = HEAD_DIM ** -0.5
    nh = GROUP_HEADS
    nc = kc_ref.shape[1]
    ck = 64
    qt = qt_ref[0].astype(F32)
    kc = kc_ref[0].astype(BF16)
    vct = vct_ref[0].astype(BF16)
    zeros = jnp.zeros((HEAD_DIM, tq), BF16)
    t_row = i * tq + lax.broadcasted_iota(jnp.int32, (1, tq), 1)

    def visible(r, rows):
        cmp_end = (r + lax.broadcasted_iota(jnp.int32, (rows, tq), 0)) * NSA_CMP_STRIDE + (NSA_CMP_LEN - 1)
        return cmp_end, cmp_end <= t_row

    mx = []
    for h in range(nh):
        qs = (qt[_head_rows(h), :] * scale).astype(BF16)
        cmp_end, mask = visible(0, nc)
        sc = _mm(kc, jnp.concatenate([qs, zeros], axis=0)) - slopes[h] * (t_row - cmp_end).astype(F32)
        mx.append(jnp.max(_score(sc_ref, h, jnp.where(mask, sc, NEG_INF)), axis=0, keepdims=True))
    den = []
    for h in range(nh):
        l8 = jnp.zeros((8, tq), F32)
        for r in range(0, nc, ck):
            _, mask = visible(r, ck)
            p = jnp.where(mask, jnp.exp(sc_ref[h, r:r + ck, :] - mx[h]), 0.0)
            sc_ref[h, r:r + ck, :] = p
            l8 = l8 + jnp.sum(p.reshape(ck // 8, 8, tq), axis=0)
        den.append(jnp.maximum(jnp.sum(l8, axis=0, keepdims=True), TINY))
    pieces = [[] for _ in range(nh)]
    for r in range(0, nc, ck):
        tot = None
        for h in range(nh):
            pn = sc_ref[h, r:r + ck, :] / den[h]
            pieces[h].append(pn.astype(BF16))
            tot = pn if tot is None else tot + pn
        ps_ref[r:r + ck, :] = tot
    for h in range(nh):
        oc_ref[0, _head_rows(h), :] = _mm(vct, jnp.concatenate(pieces[h], axis=0))
    imp_t = _mm(ov_ref[...], ps_ref[...], precision=HIGHEST)
    blk = lax.broadcasted_iota(jnp.int32, (SEL_ROWS, tq), 0)
    tt = i * tq + lax.broadcasted_iota(jnp.int32, (SEL_ROWS, tq), 1)
    qblk = lax.shift_right_logical(tt, 6)
    forced = (blk == 0) | (blk == qblk) | (blk == qblk - 1)
    score = jnp.where(forced, FORCED_SCORE, imp_t)
    score = jnp.where(blk * NSA_SEL_BLOCK <= tt, score, NEG_INF)
    sel_ref[0] = _topk_bias_t(score, NSA_SEL_TOPN).astype(sel_ref.dtype)


def _nsa_select(qt, kc, vct, b, s):
    nq = s // TQ
    nc = s // NSA_CMP_STRIDE
    nsel = s // NSA_SEL_BLOCK
    assert nsel <= SEL_ROWS and NSA_SEL_TOPN <= nsel
    cs = np.arange(nc)[None, :] * NSA_CMP_STRIDE
    bs = np.arange(SEL_ROWS)[:, None] * NSA_SEL_BLOCK
    ov = np.clip(np.minimum(cs + NSA_CMP_LEN, bs + NSA_SEL_BLOCK) - np.maximum(cs, bs), 0, None) / NSA_CMP_LEN
    ov[:, nc - 1] = 0.0
    ov[nsel:] = 0.0
    return pl.pallas_call(
        functools.partial(_nsa_sel_kernel, slopes=_alibi_slopes(1), tq=TQ),
        grid=(b, nq),
        in_specs=[_qt_spec(GROUP_WIDTH, nq),
                  pl.BlockSpec((1, nc, 128), lambda bb, i: (bb, 0, 0)),
                  pl.BlockSpec((1, HEAD_DIM, nc), lambda bb, i: (bb, 0, 0)),
                  pl.BlockSpec((SEL_ROWS, nc), lambda bb, i: (0, 0))],
        out_specs=[_qt_spec(GROUP_WIDTH, nq), _qt_spec(SEL_ROWS, nq)],
        out_shape=[jax.ShapeDtypeStruct((b * nq, GROUP_WIDTH, TQ), F32),
                   jax.ShapeDtypeStruct((b * nq, SEL_ROWS, TQ), BF16)],
        scratch_shapes=[pltpu.VMEM((GROUP_HEADS, nc, TQ), F32), pltpu.VMEM((nc, TQ), F32)],
        compiler_params=_params("parallel", "arbitrary"),
        name="nsa_select",
    )(qt, kc, vct, jnp.asarray(ov.astype(np.float32)))


def _nsa_attn_kernel(qt_ref, ks_ref, vst_ref, kw_ref, vwt_ref, sel_ref, oc_ref, g_ref, kn_ref, o_ref,
                     rhs_ref, rhsw_ref, tab_ref, m_ref, acc_ref, s_ref, ot_ref, *, slopes, tq):
    i = pl.program_id(1)
    scale = HEAD_DIM ** -0.5 * LOG2E
    nh = GROUP_HEADS
    qt = qt_ref[0].astype(F32)
    key, qry = _rel_pos(tq)
    rel = (qry - key).astype(F32)
    causal = key <= qry
    frow = lax.broadcasted_iota(jnp.int32, (HEAD_DIM, tq), 0)
    qoff = lax.broadcasted_iota(jnp.int32, (1, tq), 1).astype(F32)
    zeros = jnp.zeros((HEAD_DIM, tq), BF16)
    _flash_init(m_ref, acc_ref)
    ksd = _k_tile(ks_ref, i, tq)
    for h in range(nh):
        qs = (qt[_head_rows(h), :] * scale).astype(BF16)
        slope2 = float(np.float32(slopes[h] * LOG2E))
        s1, s2, s3 = _bf16_pieces(slope2)
        t = qoff * (-slope2)
        t1 = t.astype(BF16).astype(F32)
        t2 = (t - t1).astype(BF16).astype(F32)
        t3 = t - t1 - t2
        feat = jnp.where(frow == 0, s1, jnp.where(frow == 1, s2, jnp.where(frow == 2, s3, jnp.where(
            frow == 3, t1, jnp.where(frow == 4, t2, jnp.where(frow == 5, t3, 0.0))))))
        rhs_ref[h, 0:AUG_BLK0, :] = qs
        rhs_ref[h, AUG_BLK0:AUG_POS0, :] = sel_ref[0]
        rhs_ref[h, AUG_POS0:GROUP_WIDTH, :] = feat.astype(BF16)
        rhsw_ref[h] = jnp.concatenate([qs, zeros], axis=0)
        tab_ref[h] = rel * (-slope2)
    nwin = NSA_WINDOW // tq
    vsd = _with_ones(vst_ref[i])
    mx = [_score(s_ref, h, jnp.where(causal, _mm(ksd, rhs_ref[h]), NEG_INF)) for h in range(nh)]
    win = []
    for back in range(nwin + 1):
        jb = jnp.maximum(i - back, 0)
        kwj = _k_tile(kw_ref, jb, tq)
        for h in range(nh):
            sv = _mm(kwj, rhsw_ref[h]) + tab_ref[h]
            if back == 0:
                sv = jnp.where(causal, sv, NEG_INF)
            elif back == nwin:
                sv = jnp.where(key > qry, sv, NEG_INF)
            win.append(_score(s_ref, (back + 1) * nh + h, sv))
    for h in range(nh):
        _flash_update(s_ref, h, mx[h], 0.0, vsd, h, m_ref, acc_ref)
    for back in range(nwin + 1):
        vwj = _with_ones(vwt_ref[jnp.maximum(i - back, 0)])
        outside = jnp.where(i >= back, 0.0, NEG_INF)
        for h in range(nh):
            _flash_update(s_ref, (back + 1) * nh + h, win[back * nh + h], outside - slopes[h] * LOG2E * tq * back,
                          vwj, nh + h, m_ref, acc_ref)

    kn = jnp.max(kn_ref[...], axis=0)
    nback = [_reach_tiles(rhs_ref[h, 0:AUG_BLK0, :], m_ref[h:h + 1, :], kn[1:2, 0:1], slopes[h] * LOG2E, i, tq)
             for h in range(nh)]

    def make_stage(heads):
        nc = len(heads)

        def score(j, slot):
            ksj = _k_tile(ks_ref, j, tq)
            return tuple(_score(s_ref, slot * nc + n, _mm(ksj, rhs_ref[h])) for n, h in enumerate(heads))

        def absorb(j, slot, mx):
            vsj = _with_ones(vst_ref[j])
            dj = (i - j).astype(F32)
            for n, h in enumerate(heads):
                _flash_update(s_ref, slot * nc + n, mx[n], dj * (-slopes[h] * LOG2E * tq), vsj, h, m_ref, acc_ref)
        return score, absorb, nc

    _segmented_sweep(i, nback, make_stage, s_ref.shape[0])

    for h in range(nh):
        rows = _head_rows(h)
        ot_ref[rows, :] = (g_ref[0, rows, :] * oc_ref[0, rows, :]
                           + g_ref[0, _head_rows(nh + h), :] * _normalized(acc_ref, h)
                           + g_ref[0, _head_rows(2 * nh + h), :] * _normalized(acc_ref, nh + h))
    o_ref[...] = ot_ref[...].T.astype(o_ref.dtype)


def _nsa_attend(qt, ks, vst, kw, vwt, sel, oct, gt, knorm, b, s):
    nq = s // TQ
    return pl.pallas_call(
        functools.partial(_nsa_attn_kernel, slopes=_alibi_slopes(1), tq=TQ),
        grid=(b, nq),
        in_specs=[_qt_spec(GROUP_WIDTH, nq), _k_spec(s, GROUP_WIDTH), _vt_spec(nq, HEAD_DIM),
                  _k_spec(s, 128), _vt_spec(nq, HEAD_DIM), _qt_spec(SEL_ROWS, nq), _qt_spec(GROUP_WIDTH, nq),
                  _qt_spec(3 * GROUP_WIDTH, nq), _kn_spec(nq)],
        out_specs=_row_spec(GROUP_WIDTH, nq),
        out_shape=jax.ShapeDtypeStruct((b * s, GROUP_WIDTH), BF16),
        scratch_shapes=[pltpu.VMEM((GROUP_HEADS, GROUP_WIDTH, TQ), BF16),
                        pltpu.VMEM((GROUP_HEADS, 128, TQ), BF16),
                        pltpu.VMEM((GROUP_HEADS, TQ, TQ), F32)]
        + _flash_scratch(2 * GROUP_HEADS, SWEEP_GROUP * GROUP_HEADS),
        compiler_params=_params("parallel", "arbitrary"),
        name="nsa_attend",
    )(qt, ks.reshape(b, s, GROUP_WIDTH), vst, kw.reshape(b, s, 128), vwt, sel, oct, gt, knorm)


def _diff_kernel(qt_ref, k_ref, vt_ref, lam_ref, sub_ref, kn_ref, o_ref, rhs_ref, tab_ref, m_ref, acc_ref, s_ref, ot_ref,
                 *, slopes, tq, lambda_init):
    i = pl.program_id(1)
    scale = DIFF_QK_DIM ** -0.5 * LOG2E
    nh = GROUP_HEADS
    qt = qt_ref[0].astype(F32)
    key, qry = _rel_pos(tq)
    rel = (qry - key).astype(F32)
    causal = key <= qry
    row = lax.broadcasted_iota(jnp.int32, (GROUP_WIDTH, 1), 0)
    _flash_init(m_ref, acc_ref)
    kd = _k_tile(k_ref, i, tq)
    for h in range(nh):
        lo = h * HEAD_DIM
        rhs_ref[h] = (jnp.where((row >= lo) & (row < lo + DIFF_QK_DIM), qt, 0.0) * scale).astype(BF16)
        rhs_ref[nh + h] = (jnp.where((row >= lo + DIFF_QK_DIM) & (row < lo + HEAD_DIM), qt, 0.0)
                           * scale).astype(BF16)
        tab_ref[h] = rel * (-slopes[h] * LOG2E)
    mx = [_score(s_ref, c, jnp.where(causal, _mm(kd, rhs_ref[c]) + tab_ref[c % nh], NEG_INF))
          for c in range(2 * nh)]
    for c in range(2 * nh):
        _flash_update(s_ref, c, mx[c], 0.0, _with_ones(vt_ref[i, _head_rows(c % nh), :]), c, m_ref, acc_ref)

    kn = jnp.max(kn_ref[...], axis=0)
    nback = [jnp.maximum(
        _reach_tiles(rhs_ref[h], m_ref[h:h + 1, :], kn[2:3, 2 * h:2 * h + 1], slopes[h] * LOG2E, i, tq),
        _reach_tiles(rhs_ref[nh + h], m_ref[nh + h:nh + h + 1, :], kn[2:3, 2 * h + 1:2 * h + 2],
                     slopes[h] * LOG2E, i, tq)) for h in range(nh)]

    def make_stage(heads):
        chains = tuple(heads) + tuple(nh + h for h in heads)
        nc = len(chains)

        def score(j, slot):
            kj = _k_tile(k_ref, j, tq)
            return tuple(_score(s_ref, slot * nc + n, _mm(kj, rhs_ref[c]) + tab_ref[c % nh])
                         for n, c in enumerate(chains))

        def absorb(j, slot, mx):
            dj = (i - j).astype(F32)
            for n, c in enumerate(chains):
                h = c % nh
                _flash_update(s_ref, slot * nc + n, mx[n], dj * (-slopes[h] * LOG2E * tq),
                              _with_ones(vt_ref[j, _head_rows(h), :]), c, m_ref, acc_ref)
        return score, absorb, nc

    _segmented_sweep(i, nback, make_stage, s_ref.shape[0])
    lam_p = lam_ref[...]
    lam = (jnp.exp(jnp.sum(lam_p[0:1] * lam_p[1:2], axis=-1, keepdims=True))
           - jnp.exp(jnp.sum(lam_p[2:3] * lam_p[3:4], axis=-1, keepdims=True)) + lambda_init)
    for h in range(nh):
        o = _normalized(acc_ref, h) - lam * _normalized(acc_ref, nh + h)
        ms = jnp.mean(o * o, axis=0, keepdims=True)
        ot_ref[_head_rows(h), :] = ((o * lax.rsqrt(ms + NORM_EPS)) * sub_ref[...]) * (1.0 - lambda_init)
    o_ref[...] = ot_ref[...].T.astype(o_ref.dtype)


def _diff(qt, k, vt, lam, subln, lambda_init, knorm, b, s):
    nq = s // TQ
    return pl.pallas_call(
        functools.partial(_diff_kernel, slopes=_alibi_slopes(2), tq=TQ, lambda_init=lambda_init),
        grid=(b, nq),
        in_specs=[_qt_spec(GROUP_WIDTH, nq), _k_spec(s, GROUP_WIDTH), _vt_spec(nq, GROUP_WIDTH),
                  pl.BlockSpec((4, DIFF_QK_DIM), lambda bb, i: (0, 0)),
                  pl.BlockSpec((HEAD_DIM, TQ), lambda bb, i: (0, 0)), _kn_spec(nq)],
        out_specs=_row_spec(GROUP_WIDTH, nq),
        out_shape=jax.ShapeDtypeStruct((b * s, GROUP_WIDTH), BF16),
        scratch_shapes=[pltpu.VMEM((2 * GROUP_HEADS, GROUP_WIDTH, TQ), BF16),
                        pltpu.VMEM((GROUP_HEADS, TQ, TQ), F32)]
        + _flash_scratch(2 * GROUP_HEADS, SWEEP_GROUP * 2 * GROUP_HEADS),
        compiler_params=_params("parallel", "arbitrary"),
        name="diff_attn",
    )(qt, k.reshape(b, s, GROUP_WIDTH), vt, lam, jnp.broadcast_to(subln[:, None], (HEAD_DIM, TQ)), knorm)


def _swa_kernel(qt_ref, k_ref, vt_ref, sink_ref, o_ref, rhs_ref, tab_ref, m_ref, acc_ref, s_ref, ot_ref,
                *, slopes, tq):
    i = pl.program_id(1)
    scale = HEAD_DIM ** -0.5 * LOG2E
    nh = GROUP_HEADS
    per_kv = nh // SWA_KV_HEADS
    qt = qt_ref[0].astype(F32)
    key, qry = _rel_pos(tq)
    d = qry - key
    rel = d.astype(F32)
    zeros = jnp.zeros((HEAD_DIM, tq), BF16)
    _flash_init(m_ref, acc_ref)
    kd = _k_tile(k_ref, i, tq)
    for h in range(nh):
        qs = (qt[_head_rows(h), :] * scale).astype(BF16)
        rhs_ref[h] = jnp.concatenate([qs, zeros] if h // per_kv == 0 else [zeros, qs], axis=0)
        tab_ref[h] = rel * (-slopes[h] * LOG2E)
    in_window = (d >= 0) & (d < SWA_WINDOW)
    jp = jnp.maximum(i - 1, 0)
    kp = _k_tile(k_ref, jp, tq)
    mx = [_score(s_ref, h, jnp.where(in_window, _mm(kd, rhs_ref[h]) + tab_ref[h], NEG_INF)) for h in range(nh)]
    mp = [_score(s_ref, nh + h, jnp.where(d + tq < SWA_WINDOW, _mm(kp, rhs_ref[h]) + tab_ref[h], NEG_INF))
          for h in range(nh)]
    outside = jnp.where(i >= 1, 0.0, NEG_INF)
    for h in range(nh):
        _flash_update(s_ref, h, mx[h], 0.0, _with_ones(vt_ref[i, _head_rows(h // per_kv), :]), h, m_ref, acc_ref)
    for h in range(nh):
        _flash_update(s_ref, nh + h, mp[h], outside - slopes[h] * LOG2E * tq,
                      _with_ones(vt_ref[jp, _head_rows(h // per_kv), :]), h, m_ref, acc_ref)

    for h in range(nh):
        sink = sink_ref[h:h + 1, 0:1] * LOG2E
        m_old = m_ref[h:h + 1, :]
        m_new = jnp.maximum(m_old, sink)
        alpha = jnp.exp2(m_old - m_new)
        a = acc_ref[h]
        denom = alpha * a[HEAD_DIM:HEAD_DIM + 1] + jnp.exp2(sink - m_new)
        ot_ref[_head_rows(h), :] = (alpha * a[0:HEAD_DIM]) / denom
    o_ref[...] = ot_ref[...].T.astype(o_ref.dtype)


def _swa(qt, k, vt, sinks, b, s):
    nq = s // TQ
    kvw = SWA_KV_HEADS * HEAD_DIM
    return pl.pallas_call(
        functools.partial(_swa_kernel, slopes=_alibi_slopes(3), tq=TQ),
        grid=(b, nq),
        in_specs=[_qt_spec(GROUP_WIDTH, nq), _k_spec(s, kvw), _vt_spec(nq, kvw),
                  pl.BlockSpec((8, 128), lambda bb, i: (0, 0))],
        out_specs=_row_spec(GROUP_WIDTH, nq),
        out_shape=jax.ShapeDtypeStruct((b * s, GROUP_WIDTH), BF16),
        scratch_shapes=[pltpu.VMEM((GROUP_HEADS, kvw, TQ), BF16),
                        pltpu.VMEM((GROUP_HEADS, TQ, TQ), F32)] + _flash_scratch(GROUP_HEADS, 2 * GROUP_HEADS),
        compiler_params=_params("parallel", "arbitrary"),
        name="swa",
    )(qt, k.reshape(b, s, kvw), vt, jnp.pad(jnp.broadcast_to(sinks[:, None], (GROUP_HEADS, 128)), ((0, 4), (0, 0))))


def _out_proj_kernel(x_ref, a_ref, b_ref, c_ref, d_ref, w_ref, g_ref, o_ref):
    y = _mm(jnp.concatenate([a_ref[...], b_ref[...], c_ref[...], d_ref[...]], axis=1), w_ref[...])
    o_ref[...] = x_ref[...] + _rms(y, g_ref[...])


def _out_proj(x2, mixes, w, gain):
    n, d = x2.shape
    tm = 512
    row = lambda wd: pl.BlockSpec((tm, wd), lambda i: (i, 0))
    return pl.pallas_call(
        _out_proj_kernel,
        grid=(n // tm,),
        in_specs=[row(d)] + [row(GROUP_WIDTH)] * 4 + [pl.BlockSpec((d, d), lambda i: (0, 0)),
                                                     pl.BlockSpec((1, d), lambda i: (0, 0))],
        out_specs=row(d),
        out_shape=jax.ShapeDtypeStruct((n, d), F32),
        compiler_params=_params("parallel"),
        name="out_proj",
    )(x2, *mixes, w, gain.reshape(1, d))


def _ffn_kernel(x_ref, xp_ref, gpre_ref, wg_ref, wu_ref, cw_ref, cb_ref, wd_ref, gpost_ref, o_ref,
                h_ref, hp_ref, acc_ref, *, tiles_per_seq):
    i = pl.program_id(0)
    f = pl.program_id(1)

    @pl.when(f == 0)
    def _():
        h_ref[...] = _rms(x_ref[...], gpre_ref[...]).astype(BF16)
        hp = _rms(xp_ref[...], gpre_ref[...])
        hp_ref[...] = jnp.where(i % tiles_per_seq != 0, hp, 0.0).astype(BF16)
        acc_ref[...] = jnp.zeros(acc_ref.shape, F32)

    h = h_ref[...]
    a = _mm(h, wg_ref[...])
    ap = _mm(hp_ref[...], wg_ref[...])
    row = lax.broadcasted_iota(jnp.int32, a.shape, 0)
    a1 = jnp.where(row == 0, ap[7:8], pltpu.roll(a, 1, 0))
    a2 = jnp.where(row == 0, ap[6:7], jnp.where(row == 1, ap[7:8], pltpu.roll(a, 2, 0)))
    cw = cw_ref[...]
    conv = cw[0:1] * a2 + cw[1:2] * a1 + cw[2:3] * a + cb_ref[...]
    gated = jax.nn.gelu(conv, approximate=True) * _mm(h, wu_ref[...])
    acc_ref[...] += _mm(gated.astype(BF16), wd_ref[...])

    @pl.when(f == pl.num_programs(1) - 1)
    def _():
        o_ref[...] = x_ref[...] + _rms(acc_ref[...], gpost_ref[...])


def _ffn(x2, gpre, wg, wu, cw, cb, wd, gpost, s):
    n, d = x2.shape
    dff = wg.shape[1]
    tm, tf = 1024, 1024
    halo = 8
    return pl.pallas_call(
        functools.partial(_ffn_kernel, tiles_per_seq=s // tm),
        grid=(n // tm, dff // tf),
        in_specs=[pl.BlockSpec((tm, d), lambda i, f: (i, 0)),
                  pl.BlockSpec((halo, d), lambda i, f: (jnp.maximum(i * (tm // halo) - 1, 0), 0)),
                  pl.BlockSpec((1, d), lambda i, f: (0, 0)),
                  pl.BlockSpec((d, tf), lambda i, f: (0, f)),
                  pl.BlockSpec((d, tf), lambda i, f: (0, f)),
                  pl.BlockSpec((CONV_WIDTH, tf), lambda i, f: (0, f)),
                  pl.BlockSpec((1, tf), lambda i, f: (0, f)),
                  pl.BlockSpec((tf, d), lambda i, f: (f, 0)),
                  pl.BlockSpec((1, d), lambda i, f: (0, 0))],
        out_specs=pl.BlockSpec((tm, d), lambda i, f: (i, 0)),
        out_shape=jax.ShapeDtypeStruct((n, d), F32),
        scratch_shapes=[pltpu.VMEM((tm, d), BF16), pltpu.VMEM((halo, d), BF16), pltpu.VMEM((tm, d), F32)],
        compiler_params=_params("parallel", "arbitrary"),
        name="conv_ffn",
    )(x2, x2, gpre.reshape(1, d), wg, wu, cw, cb.reshape(1, dff), wd, gpost.reshape(1, d))


def kernel(x, attn_pre_norm, attn_post_norm, ffn_pre_norm, ffn_post_norm, w_in, w_out, nsa_cmp_pos_k, nsa_cmp_w1_k, nsa_cmp_b1_k, nsa_cmp_w2_k, nsa_cmp_pos_v, nsa_cmp_w1_v, nsa_cmp_b1_v, nsa_cmp_w2_v, diff_lambda_q1, diff_lambda_k1, diff_lambda_q2, diff_lambda_k2, diff_subln, swa_sinks, ffn_w_gate, ffn_w_up, ffn_conv_w, ffn_conv_b, ffn_w_down):
    b, s, d = x.shape
    depth = w_in.shape[0]
    assert s % 1024 == 0 and s // MOBA_BLOCK <= MOBA_ROWS
    ncols, tcols = (jnp.asarray(c) for c in _in_columns())
    x2 = x.reshape(b * s, d)
    for l in range(depth):
        w_ext = jnp.pad(w_in[l], ((0, 0), (0, 1)))
        p = _in_proj(x2, attn_pre_norm[l], jnp.take(w_ext, ncols, axis=1).astype(BF16),
                     jnp.take(w_ext, tcols, axis=1).T.astype(BF16), s)
        kmean = p["kmean"].reshape(b, s // MOBA_BLOCK, GROUP_WIDTH)
        kmean = jnp.pad(kmean, ((0, 0), (0, MOBA_ROWS - s // MOBA_BLOCK), (0, 0)))
        o_moba = _moba(p["mq"], p["mk"], p["mv"], kmean, p["knorm"], b, s)
        kc = _nsa_compress(p["ncmp"][:, :HEAD_DIM], nsa_cmp_pos_k[l], nsa_cmp_w1_k[l], nsa_cmp_b1_k[l],
                           nsa_cmp_w2_k[l], b, s, False)
        vct = _nsa_compress(p["ncmp"][:, HEAD_DIM:], nsa_cmp_pos_v[l], nsa_cmp_w1_v[l], nsa_cmp_b1_v[l],
                            nsa_cmp_w2_v[l], b, s, True)
        o_cmp, sel = _nsa_select(p["nq"], kc, vct, b, s)
        o_nsa = _nsa_attend(p["nq"], p["nks"], p["nvs"], p["nkw"], p["nvw"], sel, o_cmp, p["gates"],
                            p["knorm"], b, s)
        lam = jnp.stack([diff_lambda_q1[l], diff_lambda_k1[l], diff_lambda_q2[l], diff_lambda_k2[l]])
        lambda_init = 0.8 - 0.6 * math.exp(-0.3 * l)
        o_diff = _diff(p["dq"], p["dk"], p["dv"], lam, diff_subln[l], lambda_init, p["knorm"], b, s)
        o_swa = _swa(p["sq"], p["sk"], p["sv"], swa_sinks[l], b, s)
        x2 = _out_proj(x2, (o_moba, o_nsa, o_diff, o_swa), w_out[l].astype(BF16), attn_post_norm[l])
        x2 = _ffn(x2, ffn_pre_norm[l], ffn_w_gate[l].astype(BF16), ffn_w_up[l].astype(BF16), ffn_conv_w[l],
                  ffn_conv_b[l], ffn_w_down[l].astype(BF16), ffn_post_norm[l], s)
    return x2.reshape(b, s, d)
```
